```python
import jax, jax.numpy as jnp
from jax import lax
import numpy as np

D_MODEL = 2048
BATCH = 2
SEQ = 16384
DEPTH = 2
DEC_BATCH = 16
DEC_SEQ = 2048
PAST_LEN = 128

FNET_GROUPS = 4
FNET_GROUP_DIM = 128
FNET_DIM = FNET_GROUPS * FNET_GROUP_DIM

MLA_HEADS = 6
Q_LORA = 512
KV_LORA = 512
QK_NOPE = 128
QK_ROPE = 64
V_HEAD = 128
QK_HEAD = QK_NOPE + QK_ROPE
MLA_DIM = MLA_HEADS * V_HEAD
ROPE_THETA = 10000.0
Q_BLOCK = 128

SSD_HEADS = 12
SSD_HEAD_DIM = 64
SSD_DIM = SSD_HEADS * SSD_HEAD_DIM
SSD_GROUPS = 2
SSD_HPG = SSD_HEADS // SSD_GROUPS
SSD_STATE = 128
CONV_K = 5
CHUNK = 128
CONV_DIM = SSD_DIM + 2 * SSD_GROUPS * SSD_STATE

MIX_DIM = FNET_DIM + MLA_DIM + SSD_DIM
D_FF = 4 * D_MODEL
EPS = 1e-6

IN_SPLITS = (FNET_DIM, Q_LORA, KV_LORA, QK_ROPE, SSD_DIM, CONV_DIM, SSD_HEADS, SSD_HEADS)
IN_DIM = FNET_DIM + Q_LORA + KV_LORA + QK_ROPE + SSD_DIM + CONV_DIM + 2 * SSD_HEADS

kernel_name = 'hybrid_fnet_mla_ssd_encoder'


def rmsnorm(x, g):
    xf = x.astype(jnp.float32)
    y = xf * lax.rsqrt(jnp.mean(jnp.square(xf), axis=-1, keepdims=True) + EPS)
    return (y * g.astype(jnp.float32)).astype(x.dtype)


def fourier_mix(u):
    b, s, _ = u.shape
    uf = u.astype(jnp.float32).reshape(b, s, FNET_GROUPS, FNET_GROUP_DIM)
    out = jnp.fft.fft2(uf, axes=(1, 3), norm='ortho').real
    return out.reshape(b, s, FNET_DIM).astype(u.dtype)


def rope_tables(s):
    pos = jnp.arange(s, dtype=jnp.float32)
    inv = ROPE_THETA ** (-jnp.arange(0, QK_ROPE, 2, dtype=jnp.float32) / QK_ROPE)
    ang = pos[:, None] * inv[None, :]
    return jnp.cos(ang), jnp.sin(ang)


def apply_rope(x, cos, sin):
    xf = x.astype(jnp.float32)
    x1, x2 = xf[..., : QK_ROPE // 2], xf[..., QK_ROPE // 2:]
    c, sn = cos[None, :, None, :], sin[None, :, None, :]
    return jnp.concatenate([x1 * c - x2 * sn, x2 * c + x1 * sn], axis=-1).astype(x.dtype)


def blocked_attention(q, k, v):
    b, s, h, dq = q.shape
    nb = s // Q_BLOCK
    scale = 1.0 / np.sqrt(QK_HEAD)
    qb = q.reshape(b, nb, Q_BLOCK, h, dq).transpose(1, 0, 2, 3, 4)

    def one_block(qi):
        sc = jnp.einsum('bqhd,bkhd->bhqk', qi, k).astype(jnp.float32) * scale
        p = jax.nn.softmax(sc, axis=-1).astype(v.dtype)
        return jnp.einsum('bhqk,bkhd->bqhd', p, v)

    out = lax.map(one_block, qb)
    return out.transpose(1, 0, 2, 3, 4).reshape(b, s, h * V_HEAD)


def mla_mix(c_q, c_kv, k_pe_raw, q_norm_g, w_q_up, kv_norm_g, w_kv_up):
    b, s, _ = c_q.shape
    q = (rmsnorm(c_q, q_norm_g) @ w_q_up).reshape(b, s, MLA_HEADS, QK_HEAD)
    kv = (rmsnorm(c_kv, kv_norm_g) @ w_kv_up).reshape(b, s, MLA_HEADS, QK_NOPE + V_HEAD)
    q_nope, q_pe = q[..., :QK_NOPE], q[..., QK_NOPE:]
    k_nope, v = kv[..., :QK_NOPE], kv[..., QK_NOPE:]
    cos, sin = rope_tables(s)
    q_pe = apply_rope(q_pe, cos, sin)
    k_pe = apply_rope(k_pe_raw[:, :, None, :], cos, sin)
    q = jnp.concatenate([q_nope, q_pe], axis=-1)
    k = jnp.concatenate([k_nope, jnp.broadcast_to(k_pe, (b, s, MLA_HEADS, QK_ROPE))], axis=-1)
    return blocked_attention(q, k, v)


def depthwise_conv(u, w, bias):
    out = lax.conv_general_dilated(
        u, w[:, None, :].astype(u.dtype), window_strides=(1,),
        padding=[(CONV_K // 2, CONV_K // 2)],
        dimension_numbers=('NWC', 'WIO', 'NWC'), feature_group_count=u.shape[-1])
    return out + bias.astype(u.dtype)


def segsum(a):
    t = a.shape[-1]
    cs = jnp.cumsum(a, axis=-1)
    diff = cs[..., :, None] - cs[..., None, :]
    mask = jnp.tril(jnp.ones((t, t), dtype=bool))
    return jnp.where(mask, diff, -jnp.inf)


def ssd_scan(x, dt, a, bm, cm):
    b, s, h, p = x.shape
    nc = s // CHUNK
    g, r, n = SSD_GROUPS, SSD_HPG, SSD_STATE
    xd = (x * dt[..., None]).reshape(b, nc, CHUNK, g, r, p)
    ad = (dt * a).reshape(b, nc, CHUNK, g, r).transpose(0, 1, 3, 4, 2)
    bm = bm.reshape(b, nc, CHUNK, g, n)
    cm = cm.reshape(b, nc, CHUNK, g, n)
    acum = jnp.cumsum(ad, axis=-1)
    lmat = jnp.exp(segsum(ad))
    cb = jnp.einsum('bclgn,bcsgn->bcgls', cm, bm)
    y_diag = jnp.einsum('bcgrls,bcsgrp->bclgrp', cb[:, :, :, None] * lmat, xd)
    decay_states = jnp.exp(acum[..., -1:] - acum)
    states = jnp.einsum('bclgn,bcgrl,bclgrp->bcgrpn', bm, decay_states, xd)
    chunk_decay = jnp.exp(acum[..., -1])

    def step(carry, inp):
        st, dec = inp
        return carry * dec[..., None, None] + st, carry

    init = jnp.zeros((b, g, r, p, n), jnp.float32)
    _, prev = lax.scan(step, init, (states.transpose(1, 0, 2, 3, 4, 5),
                                    chunk_decay.transpose(1, 0, 2, 3)))
    prev = prev.transpose(1, 0, 2, 3, 4, 5)
    y_off = jnp.einsum('bclgn,bcgrpn,bcgrl->bclgrp', cm, prev, jnp.exp(acum))
    return (y_diag + y_off).reshape(b, s, h, p)


def ssd_mix(z, xbc, dt_f_raw, dt_b_raw, conv_w, conv_b, dt_bias_f, dt_bias_b,
            a_log_f, a_log_b, d_skip, ssd_norm_g):
    b, s, _ = z.shape
    xbc = jax.nn.silu(depthwise_conv(xbc, conv_w, conv_b)).astype(jnp.float32)
    xs = xbc[..., :SSD_DIM].reshape(b, s, SSD_HEADS, SSD_HEAD_DIM)
    bm = xbc[..., SSD_DIM:SSD_DIM + SSD_GROUPS * SSD_STATE].reshape(b, s, SSD_GROUPS, SSD_STATE)
    cm = xbc[..., SSD_DIM + SSD_GROUPS * SSD_STATE:].reshape(b, s, SSD_GROUPS, SSD_STATE)
    dt_f = jax.nn.softplus(dt_f_raw.astype(jnp.float32) + dt_bias_f.astype(jnp.float32))
    dt_b = jax.nn.softplus(dt_b_raw.astype(jnp.float32) + dt_bias_b.astype(jnp.float32))
    a_f = -jnp.exp(a_log_f.astype(jnp.float32))
    a_b = -jnp.exp(a_log_b.astype(jnp.float32))
    flip = lambda t: jnp.flip(t, axis=1)
    y_f = ssd_scan(xs, dt_f, a_f, bm, cm)
    y_b = flip(ssd_scan(flip(xs), flip(dt_b), a_b, flip(bm), flip(cm)))
    y = y_f + y_b + xs * d_skip.astype(jnp.float32)[:, None]
    y = y.reshape(b, s, SSD_DIM) * jax.nn.silu(z.astype(jnp.float32))
    yg = y.reshape(b, s, SSD_GROUPS, SSD_DIM // SSD_GROUPS)
    yg = yg * lax.rsqrt(jnp.mean(jnp.square(yg), axis=-1, keepdims=True) + EPS)
    y = yg.reshape(b, s, SSD_DIM) * ssd_norm_g.astype(jnp.float32)
    return y.astype(z.dtype)


def encoder_layer(x, pre_mix_g, w_in, q_norm_g, w_q_up, kv_norm_g, w_kv_up, conv_w, conv_b,
                  dt_bias_f, dt_bias_b, a_log_f, a_log_b, d_skip, ssd_norm_g, w_out,
                  post_mix_g, pre_ffn_g, w_ff1, w_ff2, post_ffn_g):
    h = rmsnorm(x, pre_mix_g)
    proj = h @ w_in
    offs = np.cumsum(IN_SPLITS)[:-1].tolist()
    u_f, c_q, c_kv, k_pe, z, xbc, dtf, dtb = jnp.split(proj, offs, axis=-1)
    mix = jnp.concatenate([
        fourier_mix(u_f),
        mla_mix(c_q, c_kv, k_pe, q_norm_g, w_q_up, kv_norm_g, w_kv_up),
        ssd_mix(z, xbc, dtf, dtb, conv_w, conv_b, dt_bias_f, dt_bias_b,
                a_log_f, a_log_b, d_skip, ssd_norm_g),
    ], axis=-1) @ w_out
    x = x + rmsnorm(mix, post_mix_g)
    h = rmsnorm(x, pre_ffn_g)
    f = jnp.square(jax.nn.relu(h @ w_ff1)) @ w_ff2
    return x + rmsnorm(f, post_ffn_g)


def setup_inputs(seed: int = 0) -> dict:
    key = jax.random.key(seed)
    ks = jax.random.split(key, 24)
    f32 = jnp.float32
    nrm = lambda k, shape, scale: jax.random.normal(k, shape, f32) * scale
    gain = lambda k, dim: 1.0 + 0.02 * jax.random.normal(k, (DEPTH, dim), f32)
    dt0 = jnp.exp(jax.random.uniform(ks[12], (2, DEPTH, SSD_HEADS), f32,
                                     np.log(1e-3), np.log(1e-1)))
    dt_bias = dt0 + jnp.log(-jnp.expm1(-dt0))
    a_log = jnp.log(jax.random.uniform(ks[13], (2, DEPTH, SSD_HEADS), f32, 1.0, 16.0))
    return {
        'x_prompt': jax.random.normal(ks[0], (BATCH, SEQ, D_MODEL), f32),
        'x_sample': jax.random.normal(ks[1], (DEC_BATCH, DEC_SEQ, D_MODEL), f32),
        'pre_mix_g': gain(ks[2], D_MODEL),
        'w_in': nrm(ks[3], (DEPTH, D_MODEL, IN_DIM), D_MODEL ** -0.5),
        'q_norm_g': gain(ks[4], Q_LORA),
        'w_q_up': nrm(ks[5], (DEPTH, Q_LORA, MLA_HEADS * QK_HEAD), Q_LORA ** -0.5),
        'kv_norm_g': gain(ks[6], KV_LORA),
        'w_kv_up': nrm(ks[7], (DEPTH, KV_LORA, MLA_HEADS * (QK_NOPE + V_HEAD)), KV_LORA ** -0.5),
        'conv_w': nrm(ks[8], (DEPTH, CONV_K, CONV_DIM), CONV_K ** -0.5),
        'conv_b': nrm(ks[9], (DEPTH, CONV_DIM), 0.02),
        'dt_bias_f': dt_bias[0],
        'dt_bias_b': dt_bias[1],
        'a_log_f': a_log[0],
        'a_log_b': a_log[1],
        'd_skip': 1.0 + 0.1 * jax.random.normal(ks[10], (DEPTH, SSD_HEADS), f32),
        'ssd_norm_g': gain(ks[11], SSD_DIM),
        'w_out': nrm(ks[14], (DEPTH, MIX_DIM, D_MODEL), MIX_DIM ** -0.5),
        'post_mix_g': gain(ks[15], D_MODEL),
        'pre_ffn_g': gain(ks[16], D_MODEL),
        'w_ff1': nrm(ks[17], (DEPTH, D_MODEL, D_FF), D_MODEL ** -0.5),
        'w_ff2': nrm(ks[18], (DEPTH, D_FF, D_MODEL), D_FF ** -0.5),
        'post_ffn_g': gain(ks[19], D_MODEL),
    }


def reference(x_prompt, x_sample, pre_mix_g, w_in, q_norm_g, w_q_up, kv_norm_g, w_kv_up,
              conv_w, conv_b, dt_bias_f, dt_bias_b, a_log_f, a_log_b, d_skip, ssd_norm_g,
              w_out, post_mix_g, pre_ffn_g, w_ff1, w_ff2, post_ffn_g):
    weights = (pre_mix_g, w_in, q_norm_g, w_q_up, kv_norm_g, w_kv_up, conv_w, conv_b,
               dt_bias_f, dt_bias_b, a_log_f, a_log_b, d_skip, ssd_norm_g, w_out,
               post_mix_g, pre_ffn_g, w_ff1, w_ff2, post_ffn_g)

    def run_trunk(x):
        for l in range(DEPTH):
            x = encoder_layer(x, *[w[l] for w in weights])
        return x

    y_prompt = run_trunk(x_prompt)
    y_sample = run_trunk(x_sample)
    return (y_prompt, y_sample)
```

```python
import functools
import math

import jax
import jax.numpy as jnp
import numpy as np
from jax import lax
from jax.experimental import pallas as pl
from jax.experimental.pallas import tpu as pltpu

F32 = jnp.float32
BF16 = jnp.bfloat16

D_MODEL = 2048
FNET_GROUPS = 4
FNET_GROUP_DIM = 128
FNET_DIM = FNET_GROUPS * FNET_GROUP_DIM
MLA_HEADS = 6
Q_LORA = 512
KV_LORA = 512
QK_NOPE = 128
QK_ROPE = 64
ROPE_HALF = QK_ROPE // 2
V_HEAD = 128
QK_HEAD = QK_NOPE + QK_ROPE
QK_PAD = 256
MLA_DIM = MLA_HEADS * V_HEAD
ROPE_THETA = 10000.0
SSD_HEADS = 12
SSD_HEAD_DIM = 64
SSD_DIM = SSD_HEADS * SSD_HEAD_DIM
SSD_GROUPS = 2
SSD_HPG = SSD_HEADS // SSD_GROUPS
SSD_GROUP_DIM = SSD_DIM // SSD_GROUPS
SSD_STATE = 128
CONV_K = 5
CHUNK = 128
CONV_DIM = SSD_DIM + 2 * SSD_GROUPS * SSD_STATE
BC_DIM = SSD_GROUPS * SSD_STATE
D_FF = 4 * D_MODEL
EPS = 1e-6

LANES = 128
SMALL = LANES
SMALL_DTF = ROPE_HALF
SMALL_DTB = ROPE_HALF + SSD_HEADS
SMALL_PE2 = LANES // 2
PROJ_DIM = FNET_DIM + Q_LORA + KV_LORA + SSD_DIM + CONV_DIM + SMALL

VMEM_LIMIT = 56 * 1024 * 1024
FNET_S2_TILE = 8


def _params(*sem):
    return pltpu.CompilerParams(dimension_semantics=sem, vmem_limit_bytes=VMEM_LIMIT)


def _dot(a, b):
    return jnp.dot(a, b, preferred_element_type=F32)


def _dot_nt(a, b):
    return lax.dot_general(a, b, (((1,), (1,)), ((), ())), preferred_element_type=F32)


def _dot_tn(a, b):
    return lax.dot_general(a, b, (((0,), (0,)), ((), ())), preferred_element_type=F32)


def _rms_scale(x):
    return lax.rsqrt(jnp.mean(x * x, axis=-1, keepdims=True) + EPS)


def _split3(x):
    hi = x.astype(BF16)
    r = x - hi.astype(F32)
    mid = r.astype(BF16)
    lo = (r - mid.astype(F32)).astype(BF16)
    return hi, mid, lo


def _dot_exact_rhs01(x, m01):
    hi, mid, lo = _split3(x)
    return _dot(hi, m01) + _dot(mid, m01) + _dot(lo, m01)


def _dot_exact_lhs01(m01, x):
    hi, mid, lo = _split3(x)
    return _dot(m01, hi) + _dot(m01, mid) + _dot(m01, lo)


def _resident(shape):
    nd = len(shape)
    return pl.BlockSpec(shape, lambda *_: (0,) * nd)


_IN_WIDTHS = (FNET_DIM, Q_LORA, KV_LORA, SSD_DIM, CONV_DIM, SMALL)


def _inproj_kernel(x_ref, g_ref, w_ref, uf_ref, cq_ref, ckv_ref, z_ref, xbc_ref, small_ref):
    x = x_ref[...]
    h = (x * _rms_scale(x) * g_ref[...]).astype(BF16)
    off = 0
    for ref, width in zip((uf_ref, cq_ref, ckv_ref, z_ref, xbc_ref, small_ref), _IN_WIDTHS):
        for c0 in range(0, width, 512):
            cw = min(512, width - c0)
            ref[:, c0:c0 + cw] = _dot(h, w_ref[:, off + c0:off + c0 + cw]).astype(ref.dtype)
        off += width


def _inproj(x, g, w, tm):
    t = x.shape[0]
    dtypes = (BF16, BF16, BF16, BF16, BF16, F32)
    return pl.pallas_call(
        _inproj_kernel,
        grid=(t // tm,),
        in_specs=[pl.BlockSpec((tm, D_MODEL), lambda i: (i, 0)),
                  _resident((1, D_MODEL)),
                  _resident((D_MODEL, PROJ_DIM))],
        out_specs=[pl.BlockSpec((tm, wd), lambda i: (i, 0)) for wd in _IN_WIDTHS],
        out_shape=[jax.ShapeDtypeStruct((t, wd), dt) for wd, dt in zip(_IN_WIDTHS, dtypes)],
        compiler_params=_params("parallel"),
    )(x, g, w)


def _dft_cos_sin(n):
    k = np.arange(n)
    ang = 2.0 * np.pi * ((k[:, None] * k[None, :]) % n) / n
    return np.cos(ang), np.sin(ang)


def _fnet_stage1_kernel(u_ref, m1_ref, twr_ref, twi_ref, yr_ref, yi_ref):
    n1 = u_ref.shape[0]
    y = _dot(m1_ref[...], u_ref[...])
    for s in range(FNET_S2_TILE):
        tr = twr_ref[:, s * LANES:(s + 1) * LANES]
        ti = twi_ref[:, s * LANES:(s + 1) * LANES]
        for q in range(FNET_DIM // LANES):
            c0 = s * FNET_DIM + q * LANES
            yr = y[:n1, c0:c0 + LANES]
            yi = y[n1:, c0:c0 + LANES]
            yr_ref[:, c0:c0 + LANES] = (yr * tr - yi * ti).astype(BF16)
            yi_ref[:, c0:c0 + LANES] = (yr * ti + yi * tr).astype(BF16)


def _fnet_stage2_kernel(yr_ref, yi_ref, m2_ref, mc_ref, o_ref):
    n2 = yr_ref.shape[0]
    y = jnp.concatenate([yr_ref[...], yi_ref[...]], axis=0)
    h = _dot(m2_ref[...], y).astype(BF16)
    for g in range(FNET_GROUPS):
        c0 = g * FNET_GROUP_DIM
        hg = jnp.concatenate([h[:n2, c0:c0 + FNET_GROUP_DIM], h[n2:, c0:c0 + FNET_GROUP_DIM]], axis=1)
        o_ref[:, c0:c0 + FNET_GROUP_DIM] = _dot(hg, mc_ref[...]).astype(BF16)


def _fourier(u):
    b, s, c = u.shape
    n2 = LANES
    n1 = s // n2
    assert n1 * n2 == s and n1 % 16 == 0 and n2 % FNET_S2_TILE == 0
    c1, s1 = _dft_cos_sin(n1)
    m1 = jnp.asarray(np.concatenate([c1, -s1], axis=0), BF16)
    c2, s2 = _dft_cos_sin(n2)
    m2 = jnp.asarray(np.block([[c2, s2], [-s2, c2]]), BF16)
    cc, sc = _dft_cos_sin(FNET_GROUP_DIM)
    mc = jnp.asarray(np.concatenate([cc, sc], axis=0), BF16)
    ang = 2.0 * np.pi * (np.arange(n1)[:, None] * np.arange(n2)[None, :]) / s
    scale = 1.0 / math.sqrt(s * FNET_GROUP_DIM)
    twr = jnp.repeat(jnp.asarray(np.cos(ang) * scale, F32), LANES, axis=1)
    twi = jnp.repeat(jnp.asarray(-np.sin(ang) * scale, F32), LANES, axis=1)

    cols = FNET_S2_TILE * c
    u2 = u.reshape(b, n1, n2 * c)
    yr, yi = pl.pallas_call(
        _fnet_stage1_kernel,
        grid=(b, n2 // FNET_S2_TILE),
        in_specs=[pl.BlockSpec((None, n1, cols), lambda i, j: (i, 0, j)),
                  _resident((2 * n1, n1)),
                  pl.BlockSpec((n1, FNET_S2_TILE * LANES), lambda i, j: (0, j)),
                  pl.BlockSpec((n1, FNET_S2_TILE * LANES), lambda i, j: (0, j))],
        out_specs=[pl.BlockSpec((None, n1, cols), lambda i, j: (i, 0, j))] * 2,
        out_shape=[jax.ShapeDtypeStruct((b, n1, n2 * c), BF16)] * 2,
        compiler_params=_params("parallel", "parallel"),
    )(u2, m1, twr, twi)

    yr = yr.reshape(b, n1, n2, c)
    yi = yi.reshape(b, n1, n2, c)
    out_t = pl.pallas_call(
        _fnet_stage2_kernel,
        grid=(b, n1),
        in_specs=[pl.BlockSpec((None, None, n2, c), lambda i, j: (i, j, 0, 0)),
                  pl.BlockSpec((None, None, n2, c), lambda i, j: (i, j, 0, 0)),
                  _resident((2 * n2, 2 * n2)),
                  _resident((2 * FNET_GROUP_DIM, FNET_GROUP_DIM))],
        out_specs=pl.BlockSpec((None, None, n2, c), lambda i, j: (i, j, 0, 0)),
        out_shape=jax.ShapeDtypeStruct((b, n1, n2, c), BF16),
        compiler_params=_params("parallel", "parallel"),
    )(yr, yi, m2, mc)
    return out_t.transpose(0, 2, 1, 3).reshape(b, s, c)


def _mla_prep_kernel(cq_ref, ckv_ref, small_ref, cos_t_ref, sin_t_ref, cos_k_ref, sin_k_ref,
                     qg_ref, kvg_ref, wq_t_ref, wk_ref, wv_t_ref, q_t_ref, k_ref, v_t_ref):
    cq = cq_ref[...].astype(F32)
    cqn = (cq * _rms_scale(cq) * qg_ref[...]).astype(BF16)
    ckv = ckv_ref[...].astype(F32)
    ckvn = (ckv * _rms_scale(ckv) * kvg_ref[...]).astype(BF16)
    q_scale = math.log2(math.e) / math.sqrt(QK_HEAD)
    cos_t = cos_t_ref[...]
    sin_t = sin_t_ref[...]
    half = LANES // 2
    for h in range(MLA_HEADS):
        r0 = h * QK_PAD
        q_t = _dot_nt(wq_t_ref[r0:r0 + QK_PAD, :], cqn)
        q_t_ref[r0:r0 + QK_NOPE, :] = (q_t[:QK_NOPE] * q_scale).astype(BF16)
        blk = q_t[QK_NOPE:]
        swapped = jnp.concatenate([blk[half:], blk[:half]], axis=0)
        q_t_ref[r0 + QK_NOPE:r0 + QK_PAD, :] = ((blk * cos_t + swapped * sin_t) * q_scale).astype(BF16)

    small = small_ref[...]
    k_rope = (small * cos_k_ref[...] + pltpu.roll(small, half, axis=1) * sin_k_ref[...]).astype(BF16)
    for h in range(MLA_HEADS):
        c0 = h * QK_PAD
        k_ref[:, c0:c0 + QK_NOPE] = _dot(ckvn, wk_ref[:, h * QK_NOPE:(h + 1) * QK_NOPE]).astype(BF16)
        k_ref[:, c0 + QK_NOPE:c0 + QK_PAD] = k_rope
        v_t_ref[h * V_HEAD:(h + 1) * V_HEAD, :] = _dot_nt(
            wv_t_ref[h * V_HEAD:(h + 1) * V_HEAD, :], ckvn).astype(BF16)


def _rope_tables(s):
    pos = jnp.arange(s, dtype=F32)
    inv = ROPE_THETA ** (-jnp.arange(0, QK_ROPE, 2, dtype=F32) / QK_ROPE)
    ang = pos[:, None] * inv[None, :]
    cos, sin = jnp.cos(ang), jnp.sin(ang)
    zero = jnp.zeros_like(cos)
    cos_k = jnp.concatenate([cos, zero, cos, zero], axis=1)
    sin_k = jnp.concatenate([-sin, zero, sin, zero], axis=1)
    return cos_k, sin_k


def _mla_prep(cq, ckv, small, qg, kvg, wq_t, wk, wv_t, tm):
    b, s, _ = cq.shape
    cos_k, sin_k = _rope_tables(s)
    cos_t, sin_t = cos_k.T, sin_k.T
    tok = lambda i, j: (i, j, 0)
    tok_t = lambda i, j: (i, 0, j)
    return pl.pallas_call(
        _mla_prep_kernel,
        grid=(b, s // tm),
        in_specs=[pl.BlockSpec((None, tm, Q_LORA), tok),
                  pl.BlockSpec((None, tm, KV_LORA), tok),
                  pl.BlockSpec((None, tm, SMALL), tok),
                  pl.BlockSpec((LANES, tm), lambda i, j: (0, j)),
                  pl.BlockSpec((LANES, tm), lambda i, j: (0, j)),
                  pl.BlockSpec((tm, LANES), lambda i, j: (j, 0)),
                  pl.BlockSpec((tm, LANES), lambda i, j: (j, 0)),
                  _resident((1, Q_LORA)),
                  _resident((1, KV_LORA)),
                  _resident((MLA_HEADS * QK_PAD, Q_LORA)),
                  _resident((KV_LORA, MLA_HEADS * QK_NOPE)),
                  _resident((MLA_DIM, KV_LORA))],
        out_specs=[pl.BlockSpec((None, MLA_HEADS * QK_PAD, tm), tok_t),
                   pl.BlockSpec((None, tm, MLA_HEADS * QK_PAD), tok),
                   pl.BlockSpec((None, MLA_DIM, tm), tok_t)],
        out_shape=[jax.ShapeDtypeStruct((b, MLA_HEADS * QK_PAD, s), BF16),
                   jax.ShapeDtypeStruct((b, s, MLA_HEADS * QK_PAD), BF16),
                   jax.ShapeDtypeStruct((b, MLA_DIM, s), BF16)],
        compiler_params=_params("parallel", "parallel"),
    )(cq, ckv, small, cos_t, sin_t, cos_k, sin_k, qg, kvg, wq_t, wk, wv_t)


def _flash_kernel(q_t_ref, k_ref, v_t_ref, o_ref, m_ref, l_ref, acc_ref):
    ki = pl.program_id(3)

    @pl.when(ki == 0)
    def _():
        m_ref[...] = jnp.full(m_ref.shape, -jnp.inf, F32)
        l_ref[...] = jnp.zeros(l_ref.shape, F32)
        acc_ref[...] = jnp.zeros(acc_ref.shape, F32)

    s = _dot(k_ref[...], q_t_ref[...])
    m_prev = m_ref[...]
    m_new = jnp.maximum(m_prev, jnp.max(s, axis=0, keepdims=True))
    alpha = jnp.exp2(m_prev - m_new)
    p = jnp.exp2(s - m_new)
    l_ref[...] = alpha * l_ref[...] + jnp.sum(p, axis=0, keepdims=True)
    acc_ref[...] = alpha * acc_ref[...] + _dot(v_t_ref[...], p.astype(BF16))
    m_ref[...] = m_new

    @pl.when(ki == pl.num_programs(3) - 1)
    def _():
        o_ref[...] = (acc_ref[...] / l_ref[...]).T.astype(o_ref.dtype)


def _flash(q_t, k, v_t, tq, tk):
    b, s, _ = k.shape
    return pl.pallas_call(
        _flash_kernel,
        grid=(b, MLA_HEADS, s // tq, s // tk),
        in_specs=[pl.BlockSpec((None, QK_PAD, tq), lambda i, h, qi, ki: (i, h, qi)),
                  pl.BlockSpec((None, tk, QK_PAD), lambda i, h, qi, ki: (i, ki, h)),
                  pl.BlockSpec((None, V_HEAD, tk), lambda i, h, qi, ki: (i, h, ki))],
        out_specs=pl.BlockSpec((None, tq, V_HEAD), lambda i, h, qi, ki: (i, qi, h)),
        out_shape=jax.ShapeDtypeStruct((b, s, MLA_DIM), BF16),
        scratch_shapes=[pltpu.VMEM((1, tq), F32), pltpu.VMEM((1, tq), F32),
                        pltpu.VMEM((V_HEAD, tq), F32)],
        compiler_params=_params("parallel", "parallel", "parallel", "arbitrary"),
    )(q_t, k, v_t)


CONV_HALO = 16
CONV_PAD = 8


def _conv_kernel(cur_ref, prev_ref, next_ref, w_ref, b_ref, xs_ref, bm_ref, cm_ref, ext_ref):
    i = pl.program_id(1)
    tc = cur_ref.shape[0]
    prev = prev_ref[...].astype(F32)[CONV_HALO - CONV_PAD:]
    nxt = next_ref[...].astype(F32)[:CONV_PAD]
    ext_ref[0:CONV_PAD, :] = jnp.where(i > 0, prev, 0.0)
    ext_ref[CONV_PAD:CONV_PAD + tc, :] = cur_ref[...].astype(F32)
    ext_ref[CONV_PAD + tc:, :] = jnp.where(i < pl.num_programs(1) - 1, nxt, 0.0)
    acc = jnp.broadcast_to(b_ref[...], (tc, CONV_DIM))
    for k in range(CONV_K):
        acc = acc + w_ref[k:k + 1, :] * ext_ref[pl.ds(CONV_PAD - CONV_K // 2 + k, tc), :]
    y = acc / (1.0 + jnp.exp(-acc))
    xs_ref[...] = y[:, :SSD_DIM].astype(BF16)
    bm_ref[...] = y[:, SSD_DIM:SSD_DIM + BC_DIM].astype(BF16)
    cm_ref[...] = y[:, SSD_DIM + BC_DIM:].astype(BF16)


def _conv(xbc, w, bias, tc):
    b, s, _ = xbc.shape
    per = tc // CONV_HALO
    last = s // CONV_HALO - 1
    return pl.pallas_call(
        _conv_kernel,
        grid=(b, s // tc),
        in_specs=[pl.BlockSpec((None, tc, CONV_DIM), lambda i, j: (i, j, 0)),
                  pl.BlockSpec((None, CONV_HALO, CONV_DIM),
                               lambda i, j: (i, jnp.maximum(j * per - 1, 0), 0)),
                  pl.BlockSpec((None, CONV_HALO, CONV_DIM),
                               lambda i, j: (i, jnp.minimum((j + 1) * per, last), 0)),
                  _resident((8, CONV_DIM)),
                  _resident((1, CONV_DIM))],
        out_specs=[pl.BlockSpec((None, tc, SSD_DIM), lambda i, j: (i, j, 0)),
                   pl.BlockSpec((None, tc, BC_DIM), lambda i, j: (i, j, 0)),
                   pl.BlockSpec((None, tc, BC_DIM), lambda i, j: (i, j, 0))],
        out_shape=[jax.ShapeDtypeStruct((b, s, SSD_DIM), BF16),
                   jax.ShapeDtypeStruct((b, s, BC_DIM), BF16),
                   jax.ShapeDtypeStruct((b, s, BC_DIM), BF16)],
        scratch_shapes=[pltpu.VMEM((tc + 2 * CONV_PAD, CONV_DIM), F32)],
        compiler_params=_params("parallel", "parallel"),
    )(xbc, xbc, xbc, w, bias)


def _ssd_kernel(xs_ref, bm_ref, cm_ref, small_ref, sel_ref, e64_ref, tri_ref, bias_ref, alog_ref,
                y_ref, state_ref, *, reverse, chunks):
    @pl.when(pl.program_id(1) == 0)
    def _():
        state_ref[...] = jnp.zeros(state_ref.shape, F32)

    a_heads = -jnp.exp(alog_ref[...])
    row = lax.broadcasted_iota(jnp.int32, (CHUNK, CHUNK), 0)
    col = lax.broadcasted_iota(jnp.int32, (CHUNK, CHUNK), 1)
    mask = (row <= col) if reverse else (row >= col)
    end = 0 if reverse else CHUNK - 1
    order = range(chunks - 1, -1, -1) if reverse else range(chunks)
    for c in order:
        r0 = c * CHUNK
        raw = _dot_exact_rhs01(small_ref[r0:r0 + CHUNK, :], sel_ref[...]) + bias_ref[...]
        dt = jnp.maximum(raw, 0.0) + jnp.log1p(jnp.exp(-jnp.abs(raw)))
        g = _dot_exact_lhs01(tri_ref[...], dt * a_heads)
        g_t = g.T
        g_wide = _dot_exact_rhs01(g, e64_ref[...])
        dt_wide = _dot_exact_rhs01(dt, e64_ref[...])
        g_end = g_wide[end:end + 1, :]

        xs = xs_ref[r0:r0 + CHUNK, :].astype(F32)
        xd = xs * dt_wide
        xd_b = xd.astype(BF16)
        xdw = (xd * jnp.exp(g_end - g_wide)).astype(BF16)
        bm = bm_ref[r0:r0 + CHUNK, :]
        cm = cm_ref[r0:r0 + CHUNK, :]
        scale_off = jnp.exp(g_wide)
        for grp in range(SSD_GROUPS):
            n0 = grp * SSD_STATE
            d0 = grp * SSD_GROUP_DIM
            b_g = bm[:, n0:n0 + SSD_STATE]
            c_g = cm[:, n0:n0 + SSD_STATE]
            cb = _dot_nt(c_g, b_g)
            state = state_ref[grp]
            y_off = _dot(c_g, state.astype(BF16)) * scale_off[:, d0:d0 + SSD_GROUP_DIM]
            for r in range(SSD_HPG):
                h = grp * SSD_HPG + r
                diff = g[:, h:h + 1] - g_t[h:h + 1, :]
                w = (cb * jnp.exp(jnp.where(mask, diff, -jnp.inf))).astype(BF16)
                p0 = h * SSD_HEAD_DIM
                y_h = _dot(w, xd_b[:, p0:p0 + SSD_HEAD_DIM])
                q0 = r * SSD_HEAD_DIM
                y_ref[r0:r0 + CHUNK, p0:p0 + SSD_HEAD_DIM] = (
                    y_h + y_off[:, q0:q0 + SSD_HEAD_DIM]).astype(y_ref.dtype)
            state_ref[grp] = (state * jnp.exp(g_end[:, d0:d0 + SSD_GROUP_DIM])
                              + _dot_tn(b_g, xdw[:, d0:d0 + SSD_GROUP_DIM]))


def _ssd_consts(reverse):
    sel = np.zeros((LANES, LANES), np.float32)
    base = SMALL_DTB if reverse else SMALL_DTF
    sel[base + np.arange(SSD_HEADS), np.arange(SSD_HEADS)] = 1.0
    e64 = np.zeros((LANES, SSD_DIM), np.float32)
    for h in range(SSD_HEADS):
        e64[h, h * SSD_HEAD_DIM:(h + 1) * SSD_HEAD_DIM] = 1.0
    tri = np.tril(np.ones((CHUNK, CHUNK), np.float32))
    if reverse:
        tri = tri.T
    return jnp.asarray(sel, BF16), jnp.asarray(e64, BF16), jnp.asarray(tri, BF16)


def _ssd(xs, bm, cm, small, dt_bias, a_log, reverse, chunks):
    b, s, _ = xs.shape
    rows = chunks * CHUNK
    n = s // rows
    sel, e64, tri = _ssd_consts(reverse)
    pad = lambda v: jnp.pad(v.astype(F32), (0, LANES - SSD_HEADS)).reshape(1, LANES)
    tok = (lambda i, j: (i, n - 1 - j, 0)) if reverse else (lambda i, j: (i, j, 0))
    return pl.pallas_call(
        functools.partial(_ssd_kernel, reverse=reverse, chunks=chunks),
        grid=(b, n),
        in_specs=[pl.BlockSpec((None, rows, SSD_DIM), tok),
                  pl.BlockSpec((None, rows, BC_DIM), tok),
                  pl.BlockSpec((None, rows, BC_DIM), tok),
                  pl.BlockSpec((None, rows, SMALL), tok),
                  _resident((LANES, LANES)),
                  _resident((LANES, SSD_DIM)),
                  _resident((CHUNK, CHUNK)),
                  _resident((1, LANES)),
                  _resident((1, LANES))],
        out_specs=pl.BlockSpec((None, rows, SSD_DIM), tok),
        out_shape=jax.ShapeDtypeStruct((b, s, SSD_DIM), BF16),
        scratch_shapes=[pltpu.VMEM((SSD_GROUPS, SSD_STATE, SSD_GROUP_DIM), F32)],
        compiler_params=_params("parallel", "arbitrary"),
    )(xs, bm, cm, small, sel, e64, tri, pad(dt_bias), pad(a_log))


def _outproj_kernel(x_ref, fo_ref, mo_ref, yf_ref, yb_ref, xs_ref, z_ref, dskip_ref, ng_ref,
                    w_ref, pg_ref, o_ref):
    y = (yf_ref[...].astype(F32) + yb_ref[...].astype(F32)
         + xs_ref[...].astype(F32) * dskip_ref[...])
    z = z_ref[...].astype(F32)
    y = y * (z / (1.0 + jnp.exp(-z)))
    parts = []
    for grp in range(SSD_GROUPS):
        yg = y[:, grp * SSD_GROUP_DIM:(grp + 1) * SSD_GROUP_DIM]
        parts.append(yg * _rms_scale(yg))
    y = (jnp.concatenate(parts, axis=1) * ng_ref[...]).astype(BF16)
    mix = (_dot(fo_ref[...], w_ref[0:FNET_DIM, :])
           + _dot(mo_ref[...], w_ref[FNET_DIM:FNET_DIM + MLA_DIM, :])
           + _dot(y, w_ref[FNET_DIM + MLA_DIM:, :]))
    o_ref[...] = x_ref[...] + mix * _rms_scale(mix) * pg_ref[...]


def _outproj(x, fo, mo, yf, yb, xs, z, dskip, ng, w, pg, tm):
    t = x.shape[0]
    rows = lambda wd: pl.BlockSpec((tm, wd), lambda i: (i, 0))
    return pl.pallas_call(
        _outproj_kernel,
        grid=(t // tm,),
        in_specs=[rows(D_MODEL), rows(FNET_DIM), rows(MLA_DIM), rows(SSD_DIM), rows(SSD_DIM),
                  rows(SSD_DIM), rows(SSD_DIM), _resident((1, SSD_DIM)), _resident((1, SSD_DIM)),
                  _resident((D_MODEL, D_MODEL)), _resident((1, D_MODEL))],
        out_specs=rows(D_MODEL),
        out_shape=jax.ShapeDtypeStruct((t, D_MODEL), F32),
        compiler_params=_params("parallel"),
    )(x, fo, mo, yf, yb, xs, z, dskip, ng, w, pg)


def _ffn_kernel(x_ref, g_ref, w1_ref, w2_ref, pg_ref, o_ref, h_ref, acc_ref):
    j = pl.program_id(1)

    @pl.when(j == 0)
    def _():
        x = x_ref[...]
        h_ref[...] = (x * _rms_scale(x) * g_ref[...]).astype(BF16)
        acc_ref[...] = jnp.zeros(acc_ref.shape, F32)

    a = jnp.maximum(_dot(h_ref[...], w1_ref[...]), 0.0)
    acc_ref[...] += _dot((a * a).astype(BF16), w2_ref[...])

    @pl.when(j == pl.num_programs(1) - 1)
    def _():
        f = acc_ref[...]
        o_ref[...] = x_ref[...] + f * _rms_scale(f) * pg_ref[...]


def _ffn(x, g, w1, w2, pg, tm, tf):
    t = x.shape[0]
    return pl.pallas_call(
        _ffn_kernel,
        grid=(t // tm, D_FF // tf),
        in_specs=[pl.BlockSpec((tm, D_MODEL), lambda i, j: (i, 0)),
                  _resident((1, D_MODEL)),
                  pl.BlockSpec((D_MODEL, tf), lambda i, j: (0, j)),
                  pl.BlockSpec((tf, D_MODEL), lambda i, j: (j, 0)),
                  _resident((1, D_MODEL))],
        out_specs=pl.BlockSpec((tm, D_MODEL), lambda i, j: (i, 0)),
        out_shape=jax.ShapeDtypeStruct((t, D_MODEL), F32),
        scratch_shapes=[pltpu.VMEM((tm, D_MODEL), BF16), pltpu.VMEM((tm, D_MODEL), F32)],
        compiler_params=_params("parallel", "arbitrary"),
    )(x, g, w1, w2, pg)


def _prep_layer(pre_mix_g, w_in, q_norm_g, w_q_up, kv_norm_g, w_kv_up, conv_w, conv_b,
                dt_bias_f, dt_bias_b, a_log_f, a_log_b, d_skip, ssd_norm_g, w_out,
                post_mix_g, pre_ffn_g, w_ff1, w_ff2, post_ffn_g):
    o_cq = FNET_DIM
    o_ckv = o_cq + Q_LORA
    o_pe = o_ckv + KV_LORA
    o_z = o_pe + QK_ROPE
    o_xbc = o_z + SSD_DIM
    o_dt = o_xbc + CONV_DIM
    zeros = lambda n: jnp.zeros((D_MODEL, n), w_in.dtype)
    small_cols = jnp.concatenate([
        w_in[:, o_pe:o_pe + ROPE_HALF],
        w_in[:, o_dt:o_dt + 2 * SSD_HEADS],
        zeros(SMALL_PE2 - ROPE_HALF - 2 * SSD_HEADS),
        w_in[:, o_pe + ROPE_HALF:o_pe + QK_ROPE],
        zeros(LANES - SMALL_PE2 - ROPE_HALF)], axis=1)
    w_in_r = jnp.concatenate([w_in[:, :o_pe], w_in[:, o_z:o_dt], small_cols], axis=1).astype(BF16)

    wq = w_q_up.reshape(Q_LORA, MLA_HEADS, QK_HEAD)
    zq = jnp.zeros((Q_LORA, MLA_HEADS, ROPE_HALF), w_q_up.dtype)
    wq = jnp.concatenate([wq[..., :QK_NOPE + ROPE_HALF], zq, wq[..., QK_NOPE + ROPE_HALF:], zq], axis=-1)
    wq_t = wq.reshape(Q_LORA, MLA_HEADS * QK_PAD).T.astype(BF16)
    wkv = w_kv_up.reshape(KV_LORA, MLA_HEADS, QK_NOPE + V_HEAD)
    wk = wkv[..., :QK_NOPE].reshape(KV_LORA, MLA_HEADS * QK_NOPE).astype(BF16)
    wv_t = wkv[..., QK_NOPE:].reshape(KV_LORA, MLA_DIM).T.astype(BF16)

    row = lambda v: v.astype(F32).reshape(1, -1)
    return dict(
        pre_mix_g=row(pre_mix_g), w_in=w_in_r, q_norm_g=row(q_norm_g), kv_norm_g=row(kv_norm_g),
        wq_t=wq_t, wk=wk, wv_t=wv_t,
        conv_w=jnp.pad(conv_w.astype(F32), ((0, 8 - CONV_K), (0, 0))), conv_b=row(conv_b),
        dt_bias_f=dt_bias_f, dt_bias_b=dt_bias_b, a_log_f=a_log_f, a_log_b=a_log_b,
        d_skip=row(jnp.repeat(d_skip, SSD_HEAD_DIM)), ssd_norm_g=row(ssd_norm_g),
        w_out=w_out.astype(BF16), post_mix_g=row(post_mix_g), pre_ffn_g=row(pre_ffn_g),
        w_ff1=w_ff1.astype(BF16), w_ff2=w_ff2.astype(BF16), post_ffn_g=row(post_ffn_g))


def _tile(n, want):
    return want if n % want == 0 else n


def _layer(x, p):
    b, s, _ = x.shape
    t = b * s
    xt = x.reshape(t, D_MODEL)
    uf, cq, ckv, z, xbc, small = _inproj(xt, p["pre_mix_g"], p["w_in"], _tile(t, 512))
    small3 = small.reshape(b, s, SMALL)

    fo = _fourier(uf.reshape(b, s, FNET_DIM)).reshape(t, FNET_DIM)

    q_t, k, v_t = _mla_prep(cq.reshape(b, s, Q_LORA), ckv.reshape(b, s, KV_LORA), small3,
                            p["q_norm_g"], p["kv_norm_g"], p["wq_t"], p["wk"], p["wv_t"],
                            _tile(s, 512))
    mo = _flash(q_t, k, v_t, _tile(s, 1024), _tile(s, 1024)).reshape(t, MLA_DIM)

    xs, bm, cm = _conv(xbc.reshape(b, s, CONV_DIM), p["conv_w"], p["conv_b"], _tile(s, 512))
    chunks = 2 if s % (2 * CHUNK) == 0 else 1
    yf = _ssd(xs, bm, cm, small3, p["dt_bias_f"], p["a_log_f"], False, chunks)
    yb = _ssd(xs, bm, cm, small3, p["dt_bias_b"], p["a_log_b"], True, chunks)

    x1 = _outproj(xt, fo, mo, yf.reshape(t, SSD_DIM), yb.reshape(t, SSD_DIM),
                  xs.reshape(t, SSD_DIM), z, p["d_skip"], p["ssd_norm_g"], p["w_out"],
                  p["post_mix_g"], _tile(t, 256))
    x2 = _ffn(x1, p["pre_ffn_g"], p["w_ff1"], p["w_ff2"], p["post_ffn_g"],
              _tile(t, 512), 1024)
    return x2.reshape(b, s, D_MODEL)


def kernel(x_prompt, x_sample, pre_mix_g, w_in, q_norm_g, w_q_up, kv_norm_g, w_kv_up, conv_w, conv_b, dt_bias_f, dt_bias_b, a_log_f, a_log_b, d_skip, ssd_norm_g, w_out, post_mix_g, pre_ffn_g, w_ff1, w_ff2, post_ffn_g):
    weights = (pre_mix_g, w_in, q_norm_g, w_q_up, kv_norm_g, w_kv_up, conv_w, conv_b,
               dt_bias_f, dt_bias_b, a_log_f, a_log_b, d_skip, ssd_norm_g, w_out,
               post_mix_g, pre_ffn_g, w_ff1, w_ff2, post_ffn_g)
    depth = w_in.shape[0]
    layers = [_prep_layer(*[w[l] for w in weights]) for l in range(depth)]

    def run_trunk(x):
        for p in layers:
            x = _layer(x, p)
        return x

    return (run_trunk(x_prompt), run_trunk(x_sample))
```

```python
import functools
import math

import jax
import jax.numpy as jnp
import numpy as np
from jax import lax
from jax.experimental import pallas as pl
from jax.experimental.pallas import tpu as pltpu

F32 = jnp.float32
BF16 = jnp.bfloat16

D_MODEL = 2048
FNET_GROUPS = 4
FNET_GROUP_DIM = 128
FNET_DIM = FNET_GROUPS * FNET_GROUP_DIM
MLA_HEADS = 6
Q_LORA = 512
KV_LORA = 512
QK_NOPE = 128
QK_ROPE = 64
ROPE_HALF = QK_ROPE // 2
V_HEAD = 128
QK_HEAD = QK_NOPE + QK_ROPE
QK_PAD = 256
MLA_DIM = MLA_HEADS * V_HEAD
ROPE_THETA = 10000.0
SSD_HEADS = 12
SSD_HEAD_DIM = 64
SSD_DIM = SSD_HEADS * SSD_HEAD_DIM
SSD_GROUPS = 2
SSD_HPG = SSD_HEADS // SSD_GROUPS
SSD_GROUP_DIM = SSD_DIM // SSD_GROUPS
SSD_STATE = 128
CONV_K = 5
CHUNK = 128
CONV_DIM = SSD_DIM + 2 * SSD_GROUPS * SSD_STATE
BC_DIM = SSD_GROUPS * SSD_STATE
D_FF = 4 * D_MODEL
EPS = 1e-6

LANES = 128
F32_ROWS = 8
BF16_ROWS = 16
SMALL = LANES
SMALL_DTF = ROPE_HALF
SMALL_DTB = ROPE_HALF + SSD_HEADS
SMALL_PE2 = LANES // 2
PROJ_DIM = FNET_DIM + Q_LORA + KV_LORA + SSD_DIM + CONV_DIM + SMALL

VMEM_LIMIT = 56 * 1024 * 1024
FNET_S2_TILE = 8
FNET_K1_TILE = 4


def _params(*sem):
    return pltpu.CompilerParams(dimension_semantics=sem, vmem_limit_bytes=VMEM_LIMIT)


def _dot(a, b):
    return jnp.dot(a, b, preferred_element_type=F32)


def _dot_nt(a, b):
    return lax.dot_general(a, b, (((1,), (1,)), ((), ())), preferred_element_type=F32)


def _dot_tn(a, b):
    return lax.dot_general(a, b, (((0,), (0,)), ((), ())), preferred_element_type=F32)


def _rms_scale(x):
    return lax.rsqrt(jnp.mean(x * x, axis=-1, keepdims=True) + EPS)


def _split3(x):
    hi = x.astype(BF16)
    r = x - hi.astype(F32)
    mid = r.astype(BF16)
    lo = (r - mid.astype(F32)).astype(BF16)
    return hi, mid, lo


def _dot_exact_rhs01(x, m01):
    hi, mid, lo = _split3(x)
    return _dot(hi, m01) + _dot(mid, m01) + _dot(lo, m01)


def _dot_exact_lhs01(m01, x):
    hi, mid, lo = _split3(x)
    return _dot(m01, hi) + _dot(m01, mid) + _dot(m01, lo)


def _resident(shape):
    nd = len(shape)
    return pl.BlockSpec(shape, lambda *_: (0,) * nd, pipeline_mode=pl.Buffered(1))


_IN_WIDTHS = (FNET_DIM, Q_LORA, KV_LORA, SSD_DIM, CONV_DIM, SMALL)


def _inproj_kernel(x_ref, g_ref, w_ref, uf_ref, cq_ref, ckv_ref, z_ref, xbc_ref, small_ref):
    x = x_ref[...]
    h = (x * _rms_scale(x) * g_ref[...]).astype(BF16)
    off = 0
    for ref, width in zip((uf_ref, cq_ref, ckv_ref, z_ref, xbc_ref, small_ref), _IN_WIDTHS):
        for c0 in range(0, width, 512):
            cw = min(512, width - c0)
            ref[:, c0:c0 + cw] = _dot(h, w_ref[:, off + c0:off + c0 + cw]).astype(ref.dtype)
        off += width


def _inproj(x, g, w, tm):
    t = x.shape[0]
    dtypes = (BF16, BF16, BF16, BF16, BF16, F32)
    return pl.pallas_call(
        _inproj_kernel,
        grid=(t // tm,),
        in_specs=[pl.BlockSpec((tm, D_MODEL), lambda i: (i, 0)),
                  _resident((1, D_MODEL)),
                  _resident((D_MODEL, PROJ_DIM))],
        out_specs=[pl.BlockSpec((tm, wd), lambda i: (i, 0)) for wd in _IN_WIDTHS],
        out_shape=[jax.ShapeDtypeStruct((t, wd), dt) for wd, dt in zip(_IN_WIDTHS, dtypes)],
        compiler_params=_params("parallel"),
    )(x, g, w)


def _dft_cos_sin(n):
    k = np.arange(n)
    ang = 2.0 * np.pi * ((k[:, None] * k[None, :]) % n) / n
    return np.cos(ang), np.sin(ang)


def _fnet_stage1_kernel(u_ref, m1_ref, twr_ref, twi_ref, yr_ref, yi_ref):
    n1 = u_ref.shape[0]
    y = _dot(m1_ref[...], u_ref[...])
    for s in range(FNET_S2_TILE):
        tr = twr_ref[:, s * LANES:(s + 1) * LANES]
        ti = twi_ref[:, s * LANES:(s + 1) * LANES]
        for q in range(FNET_DIM // LANES):
            c0 = s * FNET_DIM + q * LANES
            yr = y[:n1, c0:c0 + LANES]
            yi = y[n1:, c0:c0 + LANES]
            yr_ref[:, c0:c0 + LANES] = (yr * tr - yi * ti).astype(BF16)
            yi_ref[:, c0:c0 + LANES] = (yr * ti + yi * tr).astype(BF16)


def _fnet_stage2_kernel(yr_ref, yi_ref, m2_ref, mc_ref, o_ref):
    n2 = yr_ref.shape[1]
    for k1 in range(yr_ref.shape[0]):
        y = jnp.concatenate([yr_ref[k1], yi_ref[k1]], axis=0)
        h = _dot(m2_ref[...], y).astype(BF16)
        for g in range(FNET_GROUPS):
            c0 = g * FNET_GROUP_DIM
            hg = jnp.concatenate([h[:n2, c0:c0 + FNET_GROUP_DIM], h[n2:, c0:c0 + FNET_GROUP_DIM]],
                                 axis=1)
            o_ref[k1, :, c0:c0 + FNET_GROUP_DIM] = _dot(hg, mc_ref[...]).astype(BF16)


def _fourier(u):
    b, s, c = u.shape
    n2 = LANES
    n1 = s // n2
    assert n1 * n2 == s and n1 % 16 == 0 and n2 % FNET_S2_TILE == 0
    c1, s1 = _dft_cos_sin(n1)
    m1 = jnp.asarray(np.concatenate([c1, -s1], axis=0), BF16)
    c2, s2 = _dft_cos_sin(n2)
    m2 = jnp.asarray(np.block([[c2, s2], [-s2, c2]]), BF16)
    cc, sc = _dft_cos_sin(FNET_GROUP_DIM)
    mc = jnp.asarray(np.concatenate([cc, sc], axis=0), BF16)
    ang = 2.0 * np.pi * (np.arange(n1)[:, None] * np.arange(n2)[None, :]) / s
    scale = 1.0 / math.sqrt(s * FNET_GROUP_DIM)
    twr = jnp.repeat(jnp.asarray(np.cos(ang) * scale, F32), LANES, axis=1)
    twi = jnp.repeat(jnp.asarray(-np.sin(ang) * scale, F32), LANES, axis=1)

    cols = FNET_S2_TILE * c
    u2 = u.reshape(b, n1, n2 * c)
    yr, yi = pl.pallas_call(
        _fnet_stage1_kernel,
        grid=(b, n2 // FNET_S2_TILE),
        in_specs=[pl.BlockSpec((None, n1, cols), lambda i, j: (i, 0, j)),
                  _resident((2 * n1, n1)),
                  pl.BlockSpec((n1, FNET_S2_TILE * LANES), lambda i, j: (0, j)),
                  pl.BlockSpec((n1, FNET_S2_TILE * LANES), lambda i, j: (0, j))],
        out_specs=[pl.BlockSpec((None, n1, cols), lambda i, j: (i, 0, j))] * 2,
        out_shape=[jax.ShapeDtypeStruct((b, n1, n2 * c), BF16)] * 2,
        compiler_params=_params("parallel", "parallel"),
    )(u2, m1, twr, twi)

    yr = yr.reshape(b, n1, n2, c)
    yi = yi.reshape(b, n1, n2, c)
    out_t = pl.pallas_call(
        _fnet_stage2_kernel,
        grid=(b, n1 // FNET_K1_TILE),
        in_specs=[pl.BlockSpec((None, FNET_K1_TILE, n2, c), lambda i, j: (i, j, 0, 0)),
                  pl.BlockSpec((None, FNET_K1_TILE, n2, c), lambda i, j: (i, j, 0, 0)),
                  _resident((2 * n2, 2 * n2)),
                  _resident((2 * FNET_GROUP_DIM, FNET_GROUP_DIM))],
        out_specs=pl.BlockSpec((None, FNET_K1_TILE, n2, c), lambda i, j: (i, j, 0, 0)),
        out_shape=jax.ShapeDtypeStruct((b, n1, n2, c), BF16),
        compiler_params=_params("parallel", "parallel"),
    )(yr, yi, m2, mc)
    return out_t.transpose(0, 2, 1, 3).reshape(b, s, c)


def _mla_prep_kernel(cq_ref, ckv_ref, small_ref, cos_t_ref, sin_t_ref, cos_k_ref, sin_k_ref,
                     qg_ref, kvg_ref, wq_t_ref, wk_ref, wv_t_ref, q_t_ref, k_ref, v_t_ref):
    cq = cq_ref[...].astype(F32)
    cqn = (cq * _rms_scale(cq) * qg_ref[...]).astype(BF16)
    ckv = ckv_ref[...].astype(F32)
    ckvn = (ckv * _rms_scale(ckv) * kvg_ref[...]).astype(BF16)
    q_scale = math.log2(math.e) / math.sqrt(QK_HEAD)
    cos_t = cos_t_ref[...]
    sin_t = sin_t_ref[...]
    half = LANES // 2
    for h in range(MLA_HEADS):
        r0 = h * QK_PAD
        q_t = _dot_nt(wq_t_ref[r0:r0 + QK_PAD, :], cqn)
        q_t_ref[r0:r0 + QK_NOPE, :] = (q_t[:QK_NOPE] * q_scale).astype(BF16)
        blk = q_t[QK_NOPE:]
        swapped = jnp.concatenate([blk[half:], blk[:half]], axis=0)
        q_t_ref[r0 + QK_NOPE:r0 + QK_PAD, :] = ((blk * cos_t + swapped * sin_t) * q_scale).astype(BF16)

    small = small_ref[...]
    k_rope = (small * cos_k_ref[...] + pltpu.roll(small, half, axis=1) * sin_k_ref[...]).astype(BF16)
    for h in range(MLA_HEADS):
        c0 = h * QK_PAD
        k_ref[:, c0:c0 + QK_NOPE] = _dot(ckvn, wk_ref[:, h * QK_NOPE:(h + 1) * QK_NOPE]).astype(BF16)
        k_ref[:, c0 + QK_NOPE:c0 + QK_PAD] = k_rope
        v_t_ref[h] = _dot_nt(wv_t_ref[h * V_HEAD:(h + 1) * V_HEAD, :], ckvn).astype(BF16)


def _rope_tables(s):
    pos = jnp.arange(s, dtype=F32)
    inv = ROPE_THETA ** (-jnp.arange(0, QK_ROPE, 2, dtype=F32) / QK_ROPE)
    ang = pos[:, None] * inv[None, :]
    cos, sin = jnp.cos(ang), jnp.sin(ang)
    zero = jnp.zeros_like(cos)
    cos_k = jnp.concatenate([cos, zero, cos, zero], axis=1)
    sin_k = jnp.concatenate([-sin, zero, sin, zero], axis=1)
    return cos_k, sin_k


def _mla_prep(cq, ckv, small, qg, kvg, wq_t, wk, wv_t, tm):
    b, s, _ = cq.shape
    cos_k, sin_k = _rope_tables(s)
    cos_t, sin_t = cos_k.T, sin_k.T
    tok = lambda i, j: (i, j, 0)
    tok_t = lambda i, j: (i, 0, j)
    return pl.pallas_call(
        _mla_prep_kernel,
        grid=(b, s // tm),
        in_specs=[pl.BlockSpec((None, tm, Q_LORA), tok),
                  pl.BlockSpec((None, tm, KV_LORA), tok),
                  pl.BlockSpec((None, tm, SMALL), tok),
                  pl.BlockSpec((LANES, tm), lambda i, j: (0, j)),
                  pl.BlockSpec((LANES, tm), lambda i, j: (0, j)),
                  pl.BlockSpec((tm, LANES), lambda i, j: (j, 0)),
                  pl.BlockSpec((tm, LANES), lambda i, j: (j, 0)),
                  _resident((1, Q_LORA)),
                  _resident((1, KV_LORA)),
                  _resident((MLA_HEADS * QK_PAD, Q_LORA)),
                  _resident((KV_LORA, MLA_HEADS * QK_NOPE)),
                  _resident((MLA_DIM, KV_LORA))],
        out_specs=[pl.BlockSpec((None, MLA_HEADS * QK_PAD, tm), tok_t),
                   pl.BlockSpec((None, tm, MLA_HEADS * QK_PAD), tok),
                   pl.BlockSpec((None, MLA_HEADS, None, V_HEAD, tm), lambda i, j: (i, 0, j, 0, 0))],
        out_shape=[jax.ShapeDtypeStruct((b, MLA_HEADS * QK_PAD, s), BF16),
                   jax.ShapeDtypeStruct((b, s, MLA_HEADS * QK_PAD), BF16),
                   jax.ShapeDtypeStruct((b, MLA_HEADS, s // tm, V_HEAD, tm), BF16)],
        compiler_params=_params("parallel", "parallel"),
    )(cq, ckv, small, cos_t, sin_t, cos_k, sin_k, qg, kvg, wq_t, wk, wv_t)


def _flash_kernel(q_t_ref, k_ref, v_t_ref, o_ref, sa_ref, sb_ref, pa_ref, pb_ref, m_ref, l_ref,
                  acc_ref):
    n, _, sub = v_t_ref.shape
    tq = q_t_ref.shape[1]
    m_ref[...] = jnp.full(m_ref.shape, -jnp.inf, F32)
    l_ref[...] = jnp.zeros(l_ref.shape, F32)
    acc_ref[...] = jnp.zeros(acc_ref.shape, F32)

    def scores(c):
        k0 = pl.multiple_of(c * sub, sub)
        return _dot(k_ref[pl.ds(k0, sub), :], q_t_ref[...])

    def update(s_ref, p_ref, c):
        m8 = s_ref[0:F32_ROWS, :]
        for r in range(F32_ROWS, sub, F32_ROWS):
            m8 = jnp.maximum(m8, s_ref[r:r + F32_ROWS, :])
        m_prev = m_ref[...]
        m_new = jnp.maximum(m_prev, jnp.max(m8, axis=0, keepdims=True))
        alpha = jnp.exp2(m_prev - m_new)
        m_rows = jnp.broadcast_to(m_new, (BF16_ROWS, tq))
        l_rows = jnp.zeros((F32_ROWS, tq), F32)
        for r in range(0, sub, BF16_ROWS):
            p = jnp.exp2(s_ref[r:r + BF16_ROWS, :] - m_rows)
            l_rows = l_rows + p[:F32_ROWS] + p[F32_ROWS:]
            p_ref[r:r + BF16_ROWS, :] = p.astype(BF16)
        l_ref[...] = alpha * l_ref[...] + l_rows
        acc_ref[...] = alpha * acc_ref[...] + _dot(v_t_ref[c], p_ref[...])
        m_ref[...] = m_new

    sa_ref[...] = scores(0)

    def pair(i, carry):
        c = 2 * i
        sb_ref[...] = scores(c + 1)
        update(sa_ref, pa_ref, c)
        sa_ref[...] = scores(jnp.minimum(c + 2, n - 1))
        update(sb_ref, pb_ref, c + 1)
        return carry

    lax.fori_loop(0, n // 2, pair, 0)
    l = jnp.sum(l_ref[...], axis=0, keepdims=True)
    o_ref[...] = (acc_ref[...] / l).T.astype(o_ref.dtype)


def _flash(q_t, k, v_t, tq):
    b, s, _ = k.shape
    n, sub = v_t.shape[2], v_t.shape[4]
    assert n % 2 == 0 and n * sub == s
    return pl.pallas_call(
        _flash_kernel,
        grid=(b, MLA_HEADS, s // tq),
        in_specs=[pl.BlockSpec((None, QK_PAD, tq), lambda i, h, qi: (i, h, qi)),
                  pl.BlockSpec((None, s, QK_PAD), lambda i, h, qi: (i, 0, h)),
                  pl.BlockSpec((None, None, n, V_HEAD, sub), lambda i, h, qi: (i, h, 0, 0, 0))],
        out_specs=pl.BlockSpec((None, tq, V_HEAD), lambda i, h, qi: (i, qi, h)),
        out_shape=jax.ShapeDtypeStruct((b, s, MLA_DIM), BF16),
        scratch_shapes=[pltpu.VMEM((sub, tq), F32), pltpu.VMEM((sub, tq), F32),
                        pltpu.VMEM((sub, tq), BF16), pltpu.VMEM((sub, tq), BF16),
                        pltpu.VMEM((1, tq), F32), pltpu.VMEM((F32_ROWS, tq), F32),
                        pltpu.VMEM((V_HEAD, tq), F32)],
        compiler_params=_params("parallel", "parallel", "arbitrary"),
    )(q_t, k, v_t)


CONV_HALO = 16
CONV_PAD = 8


def _conv_kernel(cur_ref, prev_ref, next_ref, w_ref, b_ref, xs_ref, bm_ref, cm_ref, ext_ref):
    i = pl.program_id(1)
    tc = cur_ref.shape[0]
    prev = prev_ref[...].astype(F32)[CONV_HALO - CONV_PAD:]
    nxt = next_ref[...].astype(F32)[:CONV_PAD]
    ext_ref[0:CONV_PAD, :] = jnp.where(i > 0, prev, 0.0)
    ext_ref[CONV_PAD:CONV_PAD + tc, :] = cur_ref[...].astype(F32)
    ext_ref[CONV_PAD + tc:, :] = jnp.where(i < pl.num_programs(1) - 1, nxt, 0.0)
    acc = jnp.broadcast_to(b_ref[...], (tc, CONV_DIM))
    for k in range(CONV_K):
        acc = acc + w_ref[k:k + 1, :] * ext_ref[pl.ds(CONV_PAD - CONV_K // 2 + k, tc), :]
    y = acc / (1.0 + jnp.exp(-acc))
    xs_ref[...] = y[:, :SSD_DIM].astype(BF16)
    bm_ref[...] = y[:, SSD_DIM:SSD_DIM + BC_DIM].astype(BF16)
    cm_ref[...] = y[:, SSD_DIM + BC_DIM:].astype(BF16)


def _conv(xbc, w, bias, tc):
    b, s, _ = xbc.shape
    per = tc // CONV_HALO
    last = s // CONV_HALO - 1
    return pl.pallas_call(
        _conv_kernel,
        grid=(b, s // tc),
        in_specs=[pl.BlockSpec((None, tc, CONV_DIM), lambda i, j: (i, j, 0)),
                  pl.BlockSpec((None, CONV_HALO, CONV_DIM),
                               lambda i, j: (i, jnp.maximum(j * per - 1, 0), 0)),
                  pl.BlockSpec((None, CONV_HALO, CONV_DIM),
                               lambda i, j: (i, jnp.minimum((j + 1) * per, last), 0)),
                  _resident((8, CONV_DIM)),
                  _resident((1, CONV_DIM))],
        out_specs=[pl.BlockSpec((None, tc, SSD_DIM), lambda i, j: (i, j, 0)),
                   pl.BlockSpec((None, tc, BC_DIM), lambda i, j: (i, j, 0)),
                   pl.BlockSpec((None, tc, BC_DIM), lambda i, j: (i, j, 0))],
        out_shape=[jax.ShapeDtypeStruct((b, s, SSD_DIM), BF16),
                   jax.ShapeDtypeStruct((b, s, BC_DIM), BF16),
                   jax.ShapeDtypeStruct((b, s, BC_DIM), BF16)],
        scratch_shapes=[pltpu.VMEM((tc + 2 * CONV_PAD, CONV_DIM), F32)],
        compiler_params=_params("parallel", "parallel"),
    )(xbc, xbc, xbc, w, bias)


class _SsdChain:
    def __init__(self, direction, r0, xs_ref, bm_ref, cm_ref, small_ref, y_ref):
        self.d, self.r0 = direction, r0
        self.rows = slice(r0, r0 + CHUNK)
        self.xs_ref, self.bm_ref, self.cm_ref, self.small_ref, self.y_ref = (
            xs_ref, bm_ref, cm_ref, small_ref, y_ref)

    def step_size(self, sel_ref, bias_ref):
        raw = _dot_exact_rhs01(self.small_ref[self.rows, :], sel_ref[self.d]) + bias_ref[self.d]
        self.dt = jnp.maximum(raw, 0.0) + jnp.log1p(jnp.exp(-jnp.abs(raw)))

    def decay_log(self, tri_ref, a_heads):
        self.g = _dot_exact_lhs01(tri_ref[self.d], self.dt * a_heads[self.d])

    def widen(self, e64_ref):
        self.g_t = self.g.T
        self.g_wide = _dot_exact_rhs01(self.g, e64_ref[...])
        dt_hi = self.dt.astype(BF16)
        dt_lo = (self.dt - dt_hi.astype(F32)).astype(BF16)
        self.dt_wide = _dot(dt_hi, e64_ref[...]) + _dot(dt_lo, e64_ref[...])

    def within_chunk(self):
        reverse = self.d == 1
        row = lax.broadcasted_iota(jnp.int32, (CHUNK, CHUNK), 0)
        col = lax.broadcasted_iota(jnp.int32, (CHUNK, CHUNK), 1)
        mask = (row <= col) if reverse else (row >= col)
        end = 0 if reverse else CHUNK - 1
        g, g_t, g_wide = self.g, self.g_t, self.g_wide
        self.g_end = g_wide[end:end + 1, :]
        xd = self.xs_ref[self.rows, :].astype(F32) * self.dt_wide
        first = lax.broadcasted_iota(jnp.int32, (CHUNK, SSD_DIM), 1) % LANES < SSD_HEAD_DIM
        xd_first = jnp.where(first, xd, 0.0).astype(BF16)
        xd_second = jnp.where(first, 0.0, xd).astype(BF16)
        self.xdw = (xd * jnp.exp(self.g_end - g_wide)).astype(BF16)
        self.scale_off = jnp.exp(g_wide)
        bm = self.bm_ref[self.rows, :]
        cm = self.cm_ref[self.rows, :]
        self.b_g, self.c_g, self.y_diag = [], [], []
        for grp in range(SSD_GROUPS):
            n0 = grp * SSD_STATE
            b_g = bm[:, n0:n0 + SSD_STATE]
            c_g = cm[:, n0:n0 + SSD_STATE]
            self.b_g.append(b_g)
            self.c_g.append(c_g)
            cb = _dot_nt(c_g, b_g)
            for pr in range(SSD_HPG // 2):
                h0 = grp * SSD_HPG + 2 * pr
                ws = []
                for h in (h0, h0 + 1):
                    diff = g[:, h:h + 1] - g_t[h:h + 1, :]
                    ws.append((cb * jnp.exp(jnp.where(mask, diff, -jnp.inf))).astype(BF16))
                p0 = h0 * SSD_HEAD_DIM
                rhs = jnp.concatenate([xd_first[:, p0:p0 + LANES], xd_second[:, p0:p0 + LANES]],
                                      axis=0)
                self.y_diag.append(_dot(jnp.concatenate(ws, axis=1), rhs))

    def across_chunks(self, state_ref):
        for grp in range(SSD_GROUPS):
            d0 = grp * SSD_GROUP_DIM
            state = state_ref[self.d, grp]
            y_off = (_dot(self.c_g[grp], state.astype(BF16))
                     * self.scale_off[:, d0:d0 + SSD_GROUP_DIM])
            for pr in range(SSD_HPG // 2):
                p0 = d0 + pr * LANES
                y = self.y_diag[grp * (SSD_HPG // 2) + pr] + y_off[:, pr * LANES:(pr + 1) * LANES]
                self.y_ref[self.rows, p0:p0 + LANES] = y.astype(self.y_ref.dtype)
            state_ref[self.d, grp] = (state * jnp.exp(self.g_end[:, d0:d0 + SSD_GROUP_DIM])
                                      + _dot_tn(self.b_g[grp], self.xdw[:, d0:d0 + SSD_GROUP_DIM]))


def _ssd_kernel(xs_f_ref, bm_f_ref, cm_f_ref, small_f_ref, xs_b_ref, bm_b_ref, cm_b_ref, small_b_ref,
                sel_ref, e64_ref, tri_ref, bias_ref, alog_ref, yf_ref, yb_ref, state_ref, *, chunks):
    @pl.when(pl.program_id(1) == 0)
    def _():
        state_ref[...] = jnp.zeros(state_ref.shape, F32)

    a_heads = -jnp.exp(alog_ref[...])
    chains = []
    for c in range(chunks):
        chains.append(_SsdChain(0, c * CHUNK, xs_f_ref, bm_f_ref, cm_f_ref, small_f_ref, yf_ref))
        chains.append(_SsdChain(1, (chunks - 1 - c) * CHUNK, xs_b_ref, bm_b_ref, cm_b_ref,
                                small_b_ref, yb_ref))
    for ch in chains:
        ch.step_size(sel_ref, bias_ref)
    for ch in chains:
        ch.decay_log(tri_ref, a_heads)
    for ch in chains:
        ch.widen(e64_ref)
    for ch in chains:
        ch.within_chunk()
    for ch in chains:
        ch.across_chunks(state_ref)


def _ssd_consts():
    sel = np.zeros((2, LANES, LANES), np.float32)
    for d, base in enumerate((SMALL_DTF, SMALL_DTB)):
        sel[d, base + np.arange(SSD_HEADS), np.arange(SSD_HEADS)] = 1.0
    e64 = np.zeros((LANES, SSD_DIM), np.float32)
    for h in range(SSD_HEADS):
        e64[h, h * SSD_HEAD_DIM:(h + 1) * SSD_HEAD_DIM] = 1.0
    lower = np.tril(np.ones((CHUNK, CHUNK), np.float32))
    tri = np.stack([lower, lower.T])
    return jnp.asarray(sel, BF16), jnp.asarray(e64, BF16), jnp.asarray(tri, BF16)


def _ssd(xs, bm, cm, small, dt_bias_f, dt_bias_b, a_log_f, a_log_b, chunks):
    b, s, _ = xs.shape
    rows = chunks * CHUNK
    n = s // rows
    sel, e64, tri = _ssd_consts()
    pad = lambda f, bk: jnp.pad(jnp.stack([f, bk]).astype(F32),
                                ((0, 0), (0, LANES - SSD_HEADS))).reshape(2, 1, LANES)
    fwd = lambda i, j: (i, j, 0)
    bwd = lambda i, j: (i, n - 1 - j, 0)
    tiles = lambda tok: [pl.BlockSpec((None, rows, SSD_DIM), tok),
                         pl.BlockSpec((None, rows, BC_DIM), tok),
                         pl.BlockSpec((None, rows, BC_DIM), tok),
                         pl.BlockSpec((None, rows, SMALL), tok)]
    return pl.pallas_call(
        functools.partial(_ssd_kernel, chunks=chunks),
        grid=(b, n),
        in_specs=tiles(fwd) + tiles(bwd) + [
            _resident((2, LANES, LANES)),
            _resident((LANES, SSD_DIM)),
            _resident((2, CHUNK, CHUNK)),
            _resident((2, 1, LANES)),
            _resident((2, 1, LANES))],
        out_specs=[pl.BlockSpec((None, rows, SSD_DIM), fwd), pl.BlockSpec((None, rows, SSD_DIM), bwd)],
        out_shape=[jax.ShapeDtypeStruct((b, s, SSD_DIM), BF16)] * 2,
        scratch_shapes=[pltpu.VMEM((2, SSD_GROUPS, SSD_STATE, SSD_GROUP_DIM), F32)],
        compiler_params=_params("parallel", "arbitrary"),
    )(xs, bm, cm, small, xs, bm, cm, small, sel, e64, tri,
      pad(dt_bias_f, dt_bias_b), pad(a_log_f, a_log_b))


def _outproj_kernel(x_ref, fo_ref, mo_ref, yf_ref, yb_ref, xs_ref, z_ref, dskip_ref, ng_ref,
                    w_ref, pg_ref, o_ref):
    y = (yf_ref[...].astype(F32) + yb_ref[...].astype(F32)
         + xs_ref[...].astype(F32) * dskip_ref[...])
    z = z_ref[...].astype(F32)
    y = y * (z / (1.0 + jnp.exp(-z)))
    parts = []
    for grp in range(SSD_GROUPS):
        yg = y[:, grp * SSD_GROUP_DIM:(grp + 1) * SSD_GROUP_DIM]
        parts.append(yg * _rms_scale(yg))
    y = (jnp.concatenate(parts, axis=1) * ng_ref[...]).astype(BF16)
    mix = (_dot(fo_ref[...], w_ref[0:FNET_DIM, :])
           + _dot(mo_ref[...], w_ref[FNET_DIM:FNET_DIM + MLA_DIM, :])
           + _dot(y, w_ref[FNET_DIM + MLA_DIM:, :]))
    o_ref[...] = x_ref[...] + mix * _rms_scale(mix) * pg_ref[...]


def _outproj(x, fo, mo, yf, yb, xs, z, dskip, ng, w, pg, tm):
    t = x.shape[0]
    rows = lambda wd: pl.BlockSpec((tm, wd), lambda i: (i, 0))
    return pl.pallas_call(
        _outproj_kernel,
        grid=(t // tm,),
        in_specs=[rows(D_MODEL), rows(FNET_DIM), rows(MLA_DIM), rows(SSD_DIM), rows(SSD_DIM),
                  rows(SSD_DIM), rows(SSD_DIM), _resident((1, SSD_DIM)), _resident((1, SSD_DIM)),
                  _resident((D_MODEL, D_MODEL)), _resident((1, D_MODEL))],
        out_specs=rows(D_MODEL),
        out_shape=jax.ShapeDtypeStruct((t, D_MODEL), F32),
        compiler_params=_params("parallel"),
    )(x, fo, mo, yf, yb, xs, z, dskip, ng, w, pg)


def _ffn_kernel(x_ref, g_ref, w1_ref, w2_ref, pg_ref, o_ref, h_ref, acc_ref):
    j = pl.program_id(1)

    @pl.when(j == 0)
    def _():
        x = x_ref[...]
        h_ref[...] = (x * _rms_scale(x) * g_ref[...]).astype(BF16)
        acc_ref[...] = jnp.zeros(acc_ref.shape, F32)

    a = jnp.maximum(_dot(h_ref[...], w1_ref[...]), 0.0)
    acc_ref[...] += _dot((a * a).astype(BF16), w2_ref[...])

    @pl.when(j == pl.num_programs(1) - 1)
    def _():
        f = acc_ref[...]
        o_ref[...] = x_ref[...] + f * _rms_scale(f) * pg_ref[...]


def _ffn(x, g, w1, w2, pg, tm, tf):
    t = x.shape[0]
    return pl.pallas_call(
        _ffn_kernel,
        grid=(t // tm, D_FF // tf),
        in_specs=[pl.BlockSpec((tm, D_MODEL), lambda i, j: (i, 0)),
                  _resident((1, D_MODEL)),
                  pl.BlockSpec((D_MODEL, tf), lambda i, j: (0, j)),
                  pl.BlockSpec((tf, D_MODEL), lambda i, j: (j, 0)),
                  _resident((1, D_MODEL))],
        out_specs=pl.BlockSpec((tm, D_MODEL), lambda i, j: (i, 0)),
        out_shape=jax.ShapeDtypeStruct((t, D_MODEL), F32),
        scratch_shapes=[pltpu.VMEM((tm, D_MODEL), BF16), pltpu.VMEM((tm, D_MODEL), F32)],
        compiler_params=_params("parallel", "arbitrary"),
    )(x, g, w1, w2, pg)


def _prep_layer(pre_mix_g, w_in, q_norm_g, w_q_up, kv_norm_g, w_kv_up, conv_w, conv_b,
                dt_bias_f, dt_bias_b, a_log_f, a_log_b, d_skip, ssd_norm_g, w_out,
                post_mix_g, pre_ffn_g, w_ff1, w_ff2, post_ffn_g):
    o_cq = FNET_DIM
    o_ckv = o_cq + Q_LORA
    o_pe = o_ckv + KV_LORA
    o_z = o_pe + QK_ROPE
    o_xbc = o_z + SSD_DIM
    o_dt = o_xbc + CONV_DIM
    zeros = lambda n: jnp.zeros((D_MODEL, n), w_in.dtype)
    small_cols = jnp.concatenate([
        w_in[:, o_pe:o_pe + ROPE_HALF],
        w_in[:, o_dt:o_dt + 2 * SSD_HEADS],
        zeros(SMALL_PE2 - ROPE_HALF - 2 * SSD_HEADS),
        w_in[:, o_pe + ROPE_HALF:o_pe + QK_ROPE],
        zeros(LANES - SMALL_PE2 - ROPE_HALF)], axis=1)
    w_in_r = jnp.concatenate([w_in[:, :o_pe], w_in[:, o_z:o_dt], small_cols], axis=1).astype(BF16)

    wq = w_q_up.reshape(Q_LORA, MLA_HEADS, QK_HEAD)
    zq = jnp.zeros((Q_LORA, MLA_HEADS, ROPE_HALF), w_q_up.dtype)
    wq = jnp.concatenate([wq[..., :QK_NOPE + ROPE_HALF], zq, wq[..., QK_NOPE + ROPE_HALF:], zq], axis=-1)
    wq_t = wq.reshape(Q_LORA, MLA_HEADS * QK_PAD).T.astype(BF16)
    wkv = w_kv_up.reshape(KV_LORA, MLA_HEADS, QK_NOPE + V_HEAD)
    wk = wkv[..., :QK_NOPE].reshape(KV_LORA, MLA_HEADS * QK_NOPE).astype(BF16)
    wv_t = wkv[..., QK_NOPE:].reshape(KV_LORA, MLA_DIM).T.astype(BF16)

    row = lambda v: v.astype(F32).reshape(1, -1)
    return dict(
        pre_mix_g=row(pre_mix_g), w_in=w_in_r, q_norm_g=row(q_norm_g), kv_norm_g=row(kv_norm_g),
        wq_t=wq_t, wk=wk, wv_t=wv_t,
        conv_w=jnp.pad(conv_w.astype(F32), ((0, 8 - CONV_K), (0, 0))), conv_b=row(conv_b),
        dt_bias_f=dt_bias_f, dt_bias_b=dt_bias_b, a_log_f=a_log_f, a_log_b=a_log_b,
        d_skip=row(jnp.repeat(d_skip, SSD_HEAD_DIM)), ssd_norm_g=row(ssd_norm_g),
        w_out=w_out.astype(BF16), post_mix_g=row(post_mix_g), pre_ffn_g=row(pre_ffn_g),
        w_ff1=w_ff1.astype(BF16), w_ff2=w_ff2.astype(BF16), post_ffn_g=row(post_ffn_g))


def _tile(n, want):
    return want if n % want == 0 else n


def _layer(x, p):
    b, s, _ = x.shape
    t = b * s
    xt = x.reshape(t, D_MODEL)
    uf, cq, ckv, z, xbc, small = _inproj(xt, p["pre_mix_g"], p["w_in"], _tile(t, 512))
    small3 = small.reshape(b, s, SMALL)

    fo = _fourier(uf.reshape(b, s, FNET_DIM)).reshape(t, FNET_DIM)

    q_t, k, v_t = _mla_prep(cq.reshape(b, s, Q_LORA), ckv.reshape(b, s, KV_LORA), small3,
                            p["q_norm_g"], p["kv_norm_g"], p["wq_t"], p["wk"], p["wv_t"],
                            _tile(s, 1024))
    mo = _flash(q_t, k, v_t, _tile(s, 512)).reshape(t, MLA_DIM)

    xs, bm, cm = _conv(xbc.reshape(b, s, CONV_DIM), p["conv_w"], p["conv_b"], _tile(s, 512))
    chunks = 4 if s % (4 * CHUNK) == 0 else 1
    yf, yb = _ssd(xs, bm, cm, small3, p["dt_bias_f"], p["dt_bias_b"], p["a_log_f"], p["a_log_b"],
                  chunks)

    x1 = _outproj(xt, fo, mo, yf.reshape(t, SSD_DIM), yb.reshape(t, SSD_DIM),
                  xs.reshape(t, SSD_DIM), z, p["d_skip"], p["ssd_norm_g"], p["w_out"],
                  p["post_mix_g"], _tile(t, 512))
    x2 = _ffn(x1, p["pre_ffn_g"], p["w_ff1"], p["w_ff2"], p["post_ffn_g"],
              _tile(t, 512), 1024)
    return x2.reshape(b, s, D_MODEL)


def kernel(x_prompt, x_sample, pre_mix_g, w_in, q_norm_g, w_q_up, kv_norm_g, w_kv_up, conv_w, conv_b, dt_bias_f, dt_bias_b, a_log_f, a_log_b, d_skip, ssd_norm_g, w_out, post_mix_g, pre_ffn_g, w_ff1, w_ff2, post_ffn_g):
    weights = (pre_mix_g, w_in, q_norm_g, w_q_up, kv_norm_g, w_kv_up, conv_w, conv_b,
               dt_bias_f, dt_bias_b, a_log_f, a_log_b, d_skip, ssd_norm_g, w_out,
               post_mix_g, pre_ffn_g, w_ff1, w_ff2, post_ffn_g)
    depth = w_in.shape[0]
    layers = [_prep_layer(*[w[l] for w in weights]) for l in range(depth)]

    def run_trunk(x):
        for p in layers:
            x = _layer(x, p)
        return x

    return (run_trunk(x_prompt), run_trunk(x_sample))
```

```python
import functools
import math

import jax
import jax.numpy as jnp
import numpy as np
from jax import lax
from jax.experimental import pallas as pl
from jax.experimental.pallas import tpu as pltpu

F32 = jnp.float32
BF16 = jnp.bfloat16

D_MODEL = 2048
FNET_GROUPS = 4
FNET_GROUP_DIM = 128
FNET_DIM = FNET_GROUPS * FNET_GROUP_DIM
MLA_HEADS = 6
Q_LORA = 512
KV_LORA = 512
QK_NOPE = 128
QK_ROPE = 64
ROPE_HALF = QK_ROPE // 2
V_HEAD = 128
V_AUG = V_HEAD + 16
QK_HEAD = QK_NOPE + QK_ROPE
QK_PAD = 256
MLA_DIM = MLA_HEADS * V_HEAD
ROPE_THETA = 10000.0
SSD_HEADS = 12
SSD_HEAD_DIM = 64
SSD_DIM = SSD_HEADS * SSD_HEAD_DIM
SSD_GROUPS = 2
SSD_HPG = SSD_HEADS // SSD_GROUPS
SSD_GROUP_DIM = SSD_DIM // SSD_GROUPS
SSD_STATE = 128
CONV_K = 5
CHUNK = 128
CONV_DIM = SSD_DIM + 2 * SSD_GROUPS * SSD_STATE
BC_DIM = SSD_GROUPS * SSD_STATE
D_FF = 4 * D_MODEL
EPS = 1e-6

LANES = 128
F32_ROWS = 8
BF16_ROWS = 16
SMALL = LANES
SMALL_DTF = ROPE_HALF
SMALL_DTB = ROPE_HALF + SSD_HEADS
SMALL_PE2 = LANES // 2
PROJ_DIM = FNET_DIM + Q_LORA + KV_LORA + SSD_DIM + CONV_DIM + SMALL

VMEM_LIMIT = 56 * 1024 * 1024
FNET_S2_TILE = 8
FNET_K1_TILE = 4


def _params(*sem):
    return pltpu.CompilerParams(dimension_semantics=sem, vmem_limit_bytes=VMEM_LIMIT)


def _dot(a, b):
    return jnp.dot(a, b, preferred_element_type=F32)


def _dot_nt(a, b):
    return lax.dot_general(a, b, (((1,), (1,)), ((), ())), preferred_element_type=F32)


def _dot_tn(a, b):
    return lax.dot_general(a, b, (((0,), (0,)), ((), ())), preferred_element_type=F32)


def _rms_scale(x):
    return lax.rsqrt(jnp.mean(x * x, axis=-1, keepdims=True) + EPS)


def _split3(x):
    hi = x.astype(BF16)
    r = x - hi.astype(F32)
    mid = r.astype(BF16)
    lo = (r - mid.astype(F32)).astype(BF16)
    return hi, mid, lo


def _dot_exact_rhs01(x, m01):
    hi, mid, lo = _split3(x)
    return _dot(hi, m01) + _dot(mid, m01) + _dot(lo, m01)


def _dot_exact_lhs01(m01, x):
    hi, mid, lo = _split3(x)
    return _dot(m01, hi) + _dot(m01, mid) + _dot(m01, lo)


def _resident(shape):
    nd = len(shape)
    return pl.BlockSpec(shape, lambda *_: (0,) * nd, pipeline_mode=pl.Buffered(1))


_IN_WIDTHS = (FNET_DIM, Q_LORA, KV_LORA, SSD_DIM, CONV_DIM, SMALL)


def _inproj_kernel(x_ref, g_ref, w_ref, uf_ref, cq_ref, ckv_ref, z_ref, xbc_ref, small_ref):
    x = x_ref[...]
    h = (x * _rms_scale(x) * g_ref[...]).astype(BF16)
    off = 0
    for ref, width in zip((uf_ref, cq_ref, ckv_ref, z_ref, xbc_ref, small_ref), _IN_WIDTHS):
        for c0 in range(0, width, 512):
            cw = min(512, width - c0)
            ref[:, c0:c0 + cw] = _dot(h, w_ref[:, off + c0:off + c0 + cw]).astype(ref.dtype)
        off += width


def _inproj(x, g, w, tm):
    t = x.shape[0]
    dtypes = (BF16, BF16, BF16, BF16, BF16, F32)
    return pl.pallas_call(
        _inproj_kernel,
        grid=(t // tm,),
        in_specs=[pl.BlockSpec((tm, D_MODEL), lambda i: (i, 0)),
                  _resident((1, D_MODEL)),
                  _resident((D_MODEL, PROJ_DIM))],
        out_specs=[pl.BlockSpec((tm, wd), lambda i: (i, 0)) for wd in _IN_WIDTHS],
        out_shape=[jax.ShapeDtypeStruct((t, wd), dt) for wd, dt in zip(_IN_WIDTHS, dtypes)],
        compiler_params=_params("parallel"),
    )(x, g, w)


def _dft_cos_sin(n):
    k = np.arange(n)
    ang = 2.0 * np.pi * ((k[:, None] * k[None, :]) % n) / n
    return np.cos(ang), np.sin(ang)


def _fnet_stage1_kernel(u_ref, m1_ref, twr_ref, twi_ref, yr_ref, yi_ref):
    n1 = u_ref.shape[0]
    y = _dot(m1_ref[...], u_ref[...])
    for s in range(FNET_S2_TILE):
        tr = twr_ref[:, s * LANES:(s + 1) * LANES]
        ti = twi_ref[:, s * LANES:(s + 1) * LANES]
        for q in range(FNET_DIM // LANES):
            c0 = s * FNET_DIM + q * LANES
            yr = y[:n1, c0:c0 + LANES]
            yi = y[n1:, c0:c0 + LANES]
            yr_ref[:, c0:c0 + LANES] = (yr * tr - yi * ti).astype(BF16)
            yi_ref[:, c0:c0 + LANES] = (yr * ti + yi * tr).astype(BF16)


def _fnet_stage2_kernel(yr_ref, yi_ref, m2_ref, mc_ref, o_ref):
    n2 = yr_ref.shape[1]
    for k1 in range(yr_ref.shape[0]):
        y = jnp.concatenate([yr_ref[k1], yi_ref[k1]], axis=0)
        h = _dot(m2_ref[...], y).astype(BF16)
        for g in range(FNET_GROUPS):
            c0 = g * FNET_GROUP_DIM
            hg = jnp.concatenate([h[:n2, c0:c0 + FNET_GROUP_DIM], h[n2:, c0:c0 + FNET_GROUP_DIM]],
                                 axis=1)
            o_ref[k1, :, c0:c0 + FNET_GROUP_DIM] = _dot(hg, mc_ref[...]).astype(BF16)


def _fourier(u):
    b, s, c = u.shape
    n2 = LANES
    n1 = s // n2
    assert n1 * n2 == s and n1 % 16 == 0 and n2 % FNET_S2_TILE == 0
    c1, s1 = _dft_cos_sin(n1)
    m1 = jnp.asarray(np.concatenate([c1, -s1], axis=0), BF16)
    c2, s2 = _dft_cos_sin(n2)
    m2 = jnp.asarray(np.block([[c2, s2], [-s2, c2]]), BF16)
    cc, sc = _dft_cos_sin(FNET_GROUP_DIM)
    mc = jnp.asarray(np.concatenate([cc, sc], axis=0), BF16)
    ang = 2.0 * np.pi * (np.arange(n1)[:, None] * np.arange(n2)[None, :]) / s
    scale = 1.0 / math.sqrt(s * FNET_GROUP_DIM)
    twr = jnp.repeat(jnp.asarray(np.cos(ang) * scale, F32), LANES, axis=1)
    twi = jnp.repeat(jnp.asarray(-np.sin(ang) * scale, F32), LANES, axis=1)

    cols = FNET_S2_TILE * c
    u2 = u.reshape(b, n1, n2 * c)
    yr, yi = pl.pallas_call(
        _fnet_stage1_kernel,
        grid=(b, n2 // FNET_S2_TILE),
        in_specs=[pl.BlockSpec((None, n1, cols), lambda i, j: (i, 0, j)),
                  _resident((2 * n1, n1)),
                  pl.BlockSpec((n1, FNET_S2_TILE * LANES), lambda i, j: (0, j)),
                  pl.BlockSpec((n1, FNET_S2_TILE * LANES), lambda i, j: (0, j))],
        out_specs=[pl.BlockSpec((None, n1, cols), lambda i, j: (i, 0, j))] * 2,
        out_shape=[jax.ShapeDtypeStruct((b, n1, n2 * c), BF16)] * 2,
        compiler_params=_params("parallel", "parallel"),
    )(u2, m1, twr, twi)

    yr = yr.reshape(b, n1, n2, c)
    yi = yi.reshape(b, n1, n2, c)
    out_t = pl.pallas_call(
        _fnet_stage2_kernel,
        grid=(b, n1 // FNET_K1_TILE),
        in_specs=[pl.BlockSpec((None, FNET_K1_TILE, n2, c), lambda i, j: (i, j, 0, 0)),
                  pl.BlockSpec((None, FNET_K1_TILE, n2, c), lambda i, j: (i, j, 0, 0)),
                  _resident((2 * n2, 2 * n2)),
                  _resident((2 * FNET_GROUP_DIM, FNET_GROUP_DIM))],
        out_specs=pl.BlockSpec((None, FNET_K1_TILE, n2, c), lambda i, j: (i, j, 0, 0)),
        out_shape=jax.ShapeDtypeStruct((b, n1, n2, c), BF16),
        compiler_params=_params("parallel", "parallel"),
    )(yr, yi, m2, mc)
    return out_t.transpose(0, 2, 1, 3).reshape(b, s, c)


def _mla_prep_kernel(cq_ref, ckv_ref, small_ref, cos_t_ref, sin_t_ref, cos_k_ref, sin_k_ref,
                     qg_ref, kvg_ref, wq_t_ref, wk_ref, wv_t_ref, q_t_ref, k_ref, v_t_ref):
    cq = cq_ref[...].astype(F32)
    cqn = (cq * _rms_scale(cq) * qg_ref[...]).astype(BF16)
    ckv = ckv_ref[...].astype(F32)
    ckvn = (ckv * _rms_scale(ckv) * kvg_ref[...]).astype(BF16)
    q_scale = math.log2(math.e) / math.sqrt(QK_HEAD)
    cos_t = cos_t_ref[...]
    sin_t = sin_t_ref[...]
    half = LANES // 2
    for h in range(MLA_HEADS):
        r0 = h * QK_PAD
        q_t = _dot_nt(wq_t_ref[r0:r0 + QK_PAD, :], cqn)
        q_t_ref[r0:r0 + QK_NOPE, :] = (q_t[:QK_NOPE] * q_scale).astype(BF16)
        blk = q_t[QK_NOPE:]
        swapped = jnp.concatenate([blk[half:], blk[:half]], axis=0)
        q_t_ref[r0 + QK_NOPE:r0 + QK_PAD, :] = ((blk * cos_t + swapped * sin_t) * q_scale).astype(BF16)

    small = small_ref[...]
    k_rope = (small * cos_k_ref[...] + pltpu.roll(small, half, axis=1) * sin_k_ref[...]).astype(BF16)
    for h in range(MLA_HEADS):
        c0 = h * QK_PAD
        k_ref[:, c0:c0 + QK_NOPE] = _dot(ckvn, wk_ref[:, h * QK_NOPE:(h + 1) * QK_NOPE]).astype(BF16)
        k_ref[:, c0 + QK_NOPE:c0 + QK_PAD] = k_rope
        v_t_ref[h, :V_HEAD, :] = _dot_nt(wv_t_ref[h * V_HEAD:(h + 1) * V_HEAD, :], ckvn).astype(BF16)
        v_t_ref[h, V_HEAD:, :] = jnp.ones((V_AUG - V_HEAD, v_t_ref.shape[2]), BF16)


def _rope_tables(s):
    pos = jnp.arange(s, dtype=F32)
    inv = ROPE_THETA ** (-jnp.arange(0, QK_ROPE, 2, dtype=F32) / QK_ROPE)
    ang = pos[:, None] * inv[None, :]
    cos, sin = jnp.cos(ang), jnp.sin(ang)
    zero = jnp.zeros_like(cos)
    cos_k = jnp.concatenate([cos, zero, cos, zero], axis=1)
    sin_k = jnp.concatenate([-sin, zero, sin, zero], axis=1)
    return cos_k, sin_k


def _mla_prep(cq, ckv, small, qg, kvg, wq_t, wk, wv_t, tm):
    b, s, _ = cq.shape
    cos_k, sin_k = _rope_tables(s)
    cos_t, sin_t = cos_k.T, sin_k.T
    tok = lambda i, j: (i, j, 0)
    tok_t = lambda i, j: (i, 0, j)
    return pl.pallas_call(
        _mla_prep_kernel,
        grid=(b, s // tm),
        in_specs=[pl.BlockSpec((None, tm, Q_LORA), tok),
                  pl.BlockSpec((None, tm, KV_LORA), tok),
                  pl.BlockSpec((None, tm, SMALL), tok),
                  pl.BlockSpec((LANES, tm), lambda i, j: (0, j)),
                  pl.BlockSpec((LANES, tm), lambda i, j: (0, j)),
                  pl.BlockSpec((tm, LANES), lambda i, j: (j, 0)),
                  pl.BlockSpec((tm, LANES), lambda i, j: (j, 0)),
                  _resident((1, Q_LORA)),
                  _resident((1, KV_LORA)),
                  _resident((MLA_HEADS * QK_PAD, Q_LORA)),
                  _resident((KV_LORA, MLA_HEADS * QK_NOPE)),
                  _resident((MLA_DIM, KV_LORA))],
        out_specs=[pl.BlockSpec((None, MLA_HEADS * QK_PAD, tm), tok_t),
                   pl.BlockSpec((None, tm, MLA_HEADS * QK_PAD), tok),
                   pl.BlockSpec((None, MLA_HEADS, None, V_AUG, tm), lambda i, j: (i, 0, j, 0, 0))],
        out_shape=[jax.ShapeDtypeStruct((b, MLA_HEADS * QK_PAD, s), BF16),
                   jax.ShapeDtypeStruct((b, s, MLA_HEADS * QK_PAD), BF16),
                   jax.ShapeDtypeStruct((b, MLA_HEADS, s // tm, V_AUG, tm), BF16)],
        compiler_params=_params("parallel", "parallel"),
    )(cq, ckv, small, cos_t, sin_t, cos_k, sin_k, qg, kvg, wq_t, wk, wv_t)


FLASH_RING = 4


def _flash_kernel(q_t_ref, k_ref, v_t_ref, o_ref, s0_ref, s1_ref, s2_ref, s3_ref, p0_ref, p1_ref,
                  m_ref, acc_ref):
    n, _, sub = v_t_ref.shape
    tq = q_t_ref.shape[1]
    s_refs = (s0_ref, s1_ref, s2_ref, s3_ref)
    p_refs = (p0_ref, p1_ref)
    m_ref[...] = jnp.full(m_ref.shape, -jnp.inf, F32)
    acc_ref[...] = jnp.zeros(acc_ref.shape, F32)

    def scores(c):
        k0 = c * sub if isinstance(c, int) else pl.multiple_of(c * sub, sub)
        return _dot(k_ref[pl.ds(k0, sub), :], q_t_ref[...])

    def softmax(s_ref, p_ref):
        m8 = s_ref[0:F32_ROWS, :]
        for r in range(F32_ROWS, sub, F32_ROWS):
            m8 = jnp.maximum(m8, s_ref[r:r + F32_ROWS, :])
        m_prev = m_ref[...]
        m_new = jnp.maximum(m_prev, jnp.max(m8, axis=0, keepdims=True))
        alpha = jnp.exp2(m_prev - m_new)
        m_rows = jnp.broadcast_to(m_new, (BF16_ROWS, tq))
        for r in range(0, sub, BF16_ROWS):
            p_ref[r:r + BF16_ROWS, :] = jnp.exp2((s_ref[r:r + BF16_ROWS, :] - m_rows).astype(BF16))
        m_ref[...] = m_new
        return alpha

    def ring_pass(c, lookahead, first):
        for j in range(FLASH_RING):
            if j < lookahead:
                s_refs[(j + 2) % FLASH_RING][...] = scores(c + j + 2)
            if first and j == 0:
                softmax(s_refs[0], p_refs[0])
                continue
            pv = _dot(v_t_ref[c + j - 1], p_refs[(j + 1) % 2][...])
            alpha = softmax(s_refs[j], p_refs[j % 2])
            acc_ref[...] = (acc_ref[...] + pv) * alpha

    s0_ref[...] = scores(0)
    s1_ref[...] = scores(1)
    last = n - FLASH_RING
    if last == 0:
        ring_pass(0, 2, True)
    else:
        ring_pass(0, FLASH_RING, True)

        def body(i, carry):
            ring_pass(i * FLASH_RING, FLASH_RING, False)
            return carry

        lax.fori_loop(1, last // FLASH_RING, body, 0)
        ring_pass(last, 2, False)
    acc = acc_ref[...] + _dot(v_t_ref[n - 1], p_refs[(n - 1) % 2][...])
    o_ref[...] = (acc[:V_HEAD] / acc[V_HEAD:V_HEAD + 1]).T.astype(o_ref.dtype)


def _flash(q_t, k, v_t, tq):
    b, s, _ = k.shape
    n, sub = v_t.shape[2], v_t.shape[4]
    assert n % FLASH_RING == 0 and n * sub == s
    return pl.pallas_call(
        _flash_kernel,
        grid=(b, MLA_HEADS, s // tq),
        in_specs=[pl.BlockSpec((None, QK_PAD, tq), lambda i, h, qi: (i, h, qi)),
                  pl.BlockSpec((None, s, QK_PAD), lambda i, h, qi: (i, 0, h)),
                  pl.BlockSpec((None, None, n, V_AUG, sub), lambda i, h, qi: (i, h, 0, 0, 0))],
        out_specs=pl.BlockSpec((None, tq, V_HEAD), lambda i, h, qi: (i, qi, h)),
        out_shape=jax.ShapeDtypeStruct((b, s, MLA_DIM), BF16),
        scratch_shapes=[pltpu.VMEM((sub, tq), F32)] * FLASH_RING
                       + [pltpu.VMEM((sub, tq), BF16)] * 2
                       + [pltpu.VMEM((1, tq), F32), pltpu.VMEM((V_AUG, tq), F32)],
        compiler_params=_params("parallel", "parallel", "arbitrary"),
    )(q_t, k, v_t)


CONV_HALO = 16
CONV_PAD = 8


def _conv_kernel(cur_ref, prev_ref, next_ref, w_ref, b_ref, xs_ref, bm_ref, cm_ref, ext_ref):
    i = pl.program_id(1)
    tc = cur_ref.shape[0]
    prev = prev_ref[...].astype(F32)[CONV_HALO - CONV_PAD:]
    nxt = next_ref[...].astype(F32)[:CONV_PAD]
    ext_ref[0:CONV_PAD, :] = jnp.where(i > 0, prev, 0.0)
    ext_ref[CONV_PAD:CONV_PAD + tc, :] = cur_ref[...].astype(F32)
    ext_ref[CONV_PAD + tc:, :] = jnp.where(i < pl.num_programs(1) - 1, nxt, 0.0)
    acc = jnp.broadcast_to(b_ref[...], (tc, CONV_DIM))
    for k in range(CONV_K):
        acc = acc + w_ref[k:k + 1, :] * ext_ref[pl.ds(CONV_PAD - CONV_K // 2 + k, tc), :]
    y = acc / (1.0 + jnp.exp(-acc))
    xs_ref[...] = y[:, :SSD_DIM].astype(BF16)
    bm_ref[...] = y[:, SSD_DIM:SSD_DIM + BC_DIM].astype(BF16)
    cm_ref[...] = y[:, SSD_DIM + BC_DIM:].astype(BF16)


def _conv(xbc, w, bias, tc):
    b, s, _ = xbc.shape
    per = tc // CONV_HALO
    last = s // CONV_HALO - 1
    return pl.pallas_call(
        _conv_kernel,
        grid=(b, s // tc),
        in_specs=[pl.BlockSpec((None, tc, CONV_DIM), lambda i, j: (i, j, 0)),
                  pl.BlockSpec((None, CONV_HALO, CONV_DIM),
                               lambda i, j: (i, jnp.maximum(j * per - 1, 0), 0)),
                  pl.BlockSpec((None, CONV_HALO, CONV_DIM),
                               lambda i, j: (i, jnp.minimum((j + 1) * per, last), 0)),
                  _resident((8, CONV_DIM)),
                  _resident((1, CONV_DIM))],
        out_specs=[pl.BlockSpec((None, tc, SSD_DIM), lambda i, j: (i, j, 0)),
                   pl.BlockSpec((None, tc, BC_DIM), lambda i, j: (i, j, 0)),
                   pl.BlockSpec((None, tc, BC_DIM), lambda i, j: (i, j, 0))],
        out_shape=[jax.ShapeDtypeStruct((b, s, SSD_DIM), BF16),
                   jax.ShapeDtypeStruct((b, s, BC_DIM), BF16),
                   jax.ShapeDtypeStruct((b, s, BC_DIM), BF16)],
        scratch_shapes=[pltpu.VMEM((tc + 2 * CONV_PAD, CONV_DIM), F32)],
        compiler_params=_params("parallel", "parallel"),
    )(xbc, xbc, xbc, w, bias)


class _SsdChain:
    def __init__(self, direction, r0, xs_ref, bm_ref, cm_ref, small_ref, y_ref):
        self.d, self.r0 = direction, r0
        self.rows = slice(r0, r0 + CHUNK)
        self.xs_ref, self.bm_ref, self.cm_ref, self.small_ref, self.y_ref = (
            xs_ref, bm_ref, cm_ref, small_ref, y_ref)

    def step_size(self, sel_ref, bias_ref):
        raw = _dot_exact_rhs01(self.small_ref[self.rows, :], sel_ref[self.d]) + bias_ref[self.d]
        self.dt = jnp.maximum(raw, 0.0) + jnp.log1p(jnp.exp(-jnp.abs(raw)))

    def decay_log(self, tri_ref, a_heads):
        self.g = _dot_exact_lhs01(tri_ref[self.d], self.dt * a_heads[self.d])

    def widen(self, e64_ref):
        self.g_t = self.g.T
        self.g_wide = _dot_exact_rhs01(self.g, e64_ref[...])
        dt_hi = self.dt.astype(BF16)
        dt_lo = (self.dt - dt_hi.astype(F32)).astype(BF16)
        self.dt_wide = _dot(dt_hi, e64_ref[...]) + _dot(dt_lo, e64_ref[...])

    def within_chunk(self):
        reverse = self.d == 1
        row = lax.broadcasted_iota(jnp.int32, (CHUNK, CHUNK), 0)
        col = lax.broadcasted_iota(jnp.int32, (CHUNK, CHUNK), 1)
        mask = (row <= col) if reverse else (row >= col)
        end = 0 if reverse else CHUNK - 1
        g, g_t, g_wide = self.g, self.g_t, self.g_wide
        self.g_end = g_wide[end:end + 1, :]
        xd = self.xs_ref[self.rows, :].astype(F32) * self.dt_wide
        first = lax.broadcasted_iota(jnp.int32, (CHUNK, SSD_DIM), 1) % LANES < SSD_HEAD_DIM
        xd_first = jnp.where(first, xd, 0.0).astype(BF16)
        xd_second = jnp.where(first, 0.0, xd).astype(BF16)
        self.xdw = (xd * jnp.exp(self.g_end - g_wide)).astype(BF16)
        self.scale_off = jnp.exp(g_wide)
        bm = self.bm_ref[self.rows, :]
        cm = self.cm_ref[self.rows, :]
        self.b_g, self.c_g, self.y_diag = [], [], []
        for grp in range(SSD_GROUPS):
            n0 = grp * SSD_STATE
            b_g = bm[:, n0:n0 + SSD_STATE]
            c_g = cm[:, n0:n0 + SSD_STATE]
            self.b_g.append(b_g)
            self.c_g.append(c_g)
            cb = _dot_nt(c_g, b_g)
            for pr in range(SSD_HPG // 2):
                h0 = grp * SSD_HPG + 2 * pr
                ws = []
                for h in (h0, h0 + 1):
                    diff = g[:, h:h + 1] - g_t[h:h + 1, :]
                    ws.append((cb * jnp.exp(jnp.where(mask, diff, -jnp.inf))).astype(BF16))
                p0 = h0 * SSD_HEAD_DIM
                rhs = jnp.concatenate([xd_first[:, p0:p0 + LANES], xd_second[:, p0:p0 + LANES]],
                                      axis=0)
                self.y_diag.append(_dot(jnp.concatenate(ws, axis=1), rhs))

    def across_chunks(self, state_ref):
        for grp in range(SSD_GROUPS):
            d0 = grp * SSD_GROUP_DIM
            state = state_ref[self.d, grp]
            y_off = (_dot(self.c_g[grp], state.astype(BF16))
                     * self.scale_off[:, d0:d0 + SSD_GROUP_DIM])
            for pr in range(SSD_HPG // 2):
                p0 = d0 + pr * LANES
                y = self.y_diag[grp * (SSD_HPG // 2) + pr] + y_off[:, pr * LANES:(pr + 1) * LANES]
                self.y_ref[self.rows, p0:p0 + LANES] = y.astype(self.y_ref.dtype)
            state_ref[self.d, grp] = (state * jnp.exp(self.g_end[:, d0:d0 + SSD_GROUP_DIM])
                                      + _dot_tn(self.b_g[grp], self.xdw[:, d0:d0 + SSD_GROUP_DIM]))


def _ssd_kernel(xs_f_ref, bm_f_ref, cm_f_ref, small_f_ref, xs_b_ref, bm_b_ref, cm_b_ref, small_b_ref,
                sel_ref, e64_ref, tri_ref, bias_ref, alog_ref, yf_ref, yb_ref, state_ref, *, chunks):
    @pl.when(pl.program_id(1) == 0)
    def _():
        state_ref[...] = jnp.zeros(state_ref.shape, F32)

    a_heads = -jnp.exp(alog_ref[...])
    chains = []
    for c in range(chunks):
        chains.append(_SsdChain(0, c * CHUNK, xs_f_ref, bm_f_ref, cm_f_ref, small_f_ref, yf_ref))
        chains.append(_SsdChain(1, (chunks - 1 - c) * CHUNK, xs_b_ref, bm_b_ref, cm_b_ref,
                                small_b_ref, yb_ref))
    for ch in chains:
        ch.step_size(sel_ref, bias_ref)
    for ch in chains:
        ch.decay_log(tri_ref, a_heads)
    for ch in chains:
        ch.widen(e64_ref)
    for ch in chains:
        ch.within_chunk()
    for ch in chains:
        ch.across_chunks(state_ref)


def _ssd_consts():
    sel = np.zeros((2, LANES, LANES), np.float32)
    for d, base in enumerate((SMALL_DTF, SMALL_DTB)):
        sel[d, base + np.arange(SSD_HEADS), np.arange(SSD_HEADS)] = 1.0
    e64 = np.zeros((LANES, SSD_DIM), np.float32)
    for h in range(SSD_HEADS):
        e64[h, h * SSD_HEAD_DIM:(h + 1) * SSD_HEAD_DIM] = 1.0
    lower = np.tril(np.ones((CHUNK, CHUNK), np.float32))
    tri = np.stack([lower, lower.T])
    return jnp.asarray(sel, BF16), jnp.asarray(e64, BF16), jnp.asarray(tri, BF16)


def _ssd(xs, bm, cm, small, dt_bias_f, dt_bias_b, a_log_f, a_log_b, chunks):
    b, s, _ = xs.shape
    rows = chunks * CHUNK
    n = s // rows
    sel, e64, tri = _ssd_consts()
    pad = lambda f, bk: jnp.pad(jnp.stack([f, bk]).astype(F32),
                                ((0, 0), (0, LANES - SSD_HEADS))).reshape(2, 1, LANES)
    fwd = lambda i, j: (i, j, 0)
    bwd = lambda i, j: (i, n - 1 - j, 0)
    tiles = lambda tok: [pl.BlockSpec((None, rows, SSD_DIM), tok),
                         pl.BlockSpec((None, rows, BC_DIM), tok),
                         pl.BlockSpec((None, rows, BC_DIM), tok),
                         pl.BlockSpec((None, rows, SMALL), tok)]
    return pl.pallas_call(
        functools.partial(_ssd_kernel, chunks=chunks),
        grid=(b, n),
        in_specs=tiles(fwd) + tiles(bwd) + [
            _resident((2, LANES, LANES)),
            _resident((LANES, SSD_DIM)),
            _resident((2, CHUNK, CHUNK)),
            _resident((2, 1, LANES)),
            _resident((2, 1, LANES))],
        out_specs=[pl.BlockSpec((None, rows, SSD_DIM), fwd), pl.BlockSpec((None, rows, SSD_DIM), bwd)],
        out_shape=[jax.ShapeDtypeStruct((b, s, SSD_DIM), BF16)] * 2,
        scratch_shapes=[pltpu.VMEM((2, SSD_GROUPS, SSD_STATE, SSD_GROUP_DIM), F32)],
        compiler_params=_params("parallel", "arbitrary"),
    )(xs, bm, cm, small, xs, bm, cm, small, sel, e64, tri,
      pad(dt_bias_f, dt_bias_b), pad(a_log_f, a_log_b))


def _outproj_kernel(x_ref, fo_ref, mo_ref, yf_ref, yb_ref, xs_ref, z_ref, dskip_ref, ng_ref,
                    w_ref, pg_ref, o_ref):
    y = (yf_ref[...].astype(F32) + yb_ref[...].astype(F32)
         + xs_ref[...].astype(F32) * dskip_ref[...])
    z = z_ref[...].astype(F32)
    y = y * (z / (1.0 + jnp.exp(-z)))
    parts = []
    for grp in range(SSD_GROUPS):
        yg = y[:, grp * SSD_GROUP_DIM:(grp + 1) * SSD_GROUP_DIM]
        parts.append(yg * _rms_scale(yg))
    y = (jnp.concatenate(parts, axis=1) * ng_ref[...]).astype(BF16)
    mix = (_dot(fo_ref[...], w_ref[0:FNET_DIM, :])
           + _dot(mo_ref[...], w_ref[FNET_DIM:FNET_DIM + MLA_DIM, :])
           + _dot(y, w_ref[FNET_DIM + MLA_DIM:, :]))
    o_ref[...] = x_ref[...] + mix * _rms_scale(mix) * pg_ref[...]


def _outproj(x, fo, mo, yf, yb, xs, z, dskip, ng, w, pg, tm):
    t = x.shape[0]
    rows = lambda wd: pl.BlockSpec((tm, wd), lambda i: (i, 0))
    return pl.pallas_call(
        _outproj_kernel,
        grid=(t // tm,),
        in_specs=[rows(D_MODEL), rows(FNET_DIM), rows(MLA_DIM), rows(SSD_DIM), rows(SSD_DIM),
                  rows(SSD_DIM), rows(SSD_DIM), _resident((1, SSD_DIM)), _resident((1, SSD_DIM)),
                  _resident((D_MODEL, D_MODEL)), _resident((1, D_MODEL))],
        out_specs=rows(D_MODEL),
        out_shape=jax.ShapeDtypeStruct((t, D_MODEL), F32),
        compiler_params=_params("parallel"),
    )(x, fo, mo, yf, yb, xs, z, dskip, ng, w, pg)


def _ffn_kernel(x_ref, g_ref, w1_ref, w2_ref, pg_ref, o_ref, h_ref, acc_ref):
    j = pl.program_id(1)

    @pl.when(j == 0)
    def _():
        x = x_ref[...]
        h_ref[...] = (x * _rms_scale(x) * g_ref[...]).astype(BF16)
        acc_ref[...] = jnp.zeros(acc_ref.shape, F32)

    a = jnp.maximum(_dot(h_ref[...], w1_ref[...]), 0.0)
    acc_ref[...] += _dot((a * a).astype(BF16), w2_ref[...])

    @pl.when(j == pl.num_programs(1) - 1)
    def _():
        f = acc_ref[...]
        o_ref[...] = x_ref[...] + f * _rms_scale(f) * pg_ref[...]


def _ffn(x, g, w1, w2, pg, tm, tf):
    t = x.shape[0]
    return pl.pallas_call(
        _ffn_kernel,
        grid=(t // tm, D_FF // tf),
        in_specs=[pl.BlockSpec((tm, D_MODEL), lambda i, j: (i, 0)),
                  _resident((1, D_MODEL)),
                  pl.BlockSpec((D_MODEL, tf), lambda i, j: (0, j)),
                  pl.BlockSpec((tf, D_MODEL), lambda i, j: (j, 0)),
                  _resident((1, D_MODEL))],
        out_specs=pl.BlockSpec((tm, D_MODEL), lambda i, j: (i, 0)),
        out_shape=jax.ShapeDtypeStruct((t, D_MODEL), F32),
        scratch_shapes=[pltpu.VMEM((tm, D_MODEL), BF16), pltpu.VMEM((tm, D_MODEL), F32)],
        compiler_params=_params("parallel", "arbitrary"),
    )(x, g, w1, w2, pg)


def _prep_layer(pre_mix_g, w_in, q_norm_g, w_q_up, kv_norm_g, w_kv_up, conv_w, conv_b,
                dt_bias_f, dt_bias_b, a_log_f, a_log_b, d_skip, ssd_norm_g, w_out,
                post_mix_g, pre_ffn_g, w_ff1, w_ff2, post_ffn_g):
    o_cq = FNET_DIM
    o_ckv = o_cq + Q_LORA
    o_pe = o_ckv + KV_LORA
    o_z = o_pe + QK_ROPE
    o_xbc = o_z + SSD_DIM
    o_dt = o_xbc + CONV_DIM
    zeros = lambda n: jnp.zeros((D_MODEL, n), w_in.dtype)
    small_cols = jnp.concatenate([
        w_in[:, o_pe:o_pe + ROPE_HALF],
        w_in[:, o_dt:o_dt + 2 * SSD_HEADS],
        zeros(SMALL_PE2 - ROPE_HALF - 2 * SSD_HEADS),
        w_in[:, o_pe + ROPE_HALF:o_pe + QK_ROPE],
        zeros(LANES - SMALL_PE2 - ROPE_HALF)], axis=1)
    w_in_r = jnp.concatenate([w_in[:, :o_pe], w_in[:, o_z:o_dt], small_cols], axis=1).astype(BF16)

    wq = w_q_up.reshape(Q_LORA, MLA_HEADS, QK_HEAD)
    zq = jnp.zeros((Q_LORA, MLA_HEADS, ROPE_HALF), w_q_up.dtype)
    wq = jnp.concatenate([wq[..., :QK_NOPE + ROPE_HALF], zq, wq[..., QK_NOPE + ROPE_HALF:], zq], axis=-1)
    wq_t = wq.reshape(Q_LORA, MLA_HEADS * QK_PAD).T.astype(BF16)
    wkv = w_kv_up.reshape(KV_LORA, MLA_HEADS, QK_NOPE + V_HEAD)
    wk = wkv[..., :QK_NOPE].reshape(KV_LORA, MLA_HEADS * QK_NOPE).astype(BF16)
    wv_t = wkv[..., QK_NOPE:].reshape(KV_LORA, MLA_DIM).T.astype(BF16)

    row = lambda v: v.astype(F32).reshape(1, -1)
    return dict(
        pre_mix_g=row(pre_mix_g), w_in=w_in_r, q_norm_g=row(q_norm_g), kv_norm_g=row(kv_norm_g),
        wq_t=wq_t, wk=wk, wv_t=wv_t,
        conv_w=jnp.pad(conv_w.astype(F32), ((0, 8 - CONV_K), (0, 0))), conv_b=row(conv_b),
        dt_bias_f=dt_bias_f, dt_bias_b=dt_bias_b, a_log_f=a_log_f, a_log_b=a_log_b,
        d_skip=row(jnp.repeat(d_skip, SSD_HEAD_DIM)), ssd_norm_g=row(ssd_norm_g),
        w_out=w_out.astype(BF16), post_mix_g=row(post_mix_g), pre_ffn_g=row(pre_ffn_g),
        w_ff1=w_ff1.astype(BF16), w_ff2=w_ff2.astype(BF16), post_ffn_g=row(post_ffn_g))


def _tile(n, want):
    return want if n % want == 0 else n


def _layer(x, p):
    b, s, _ = x.shape
    t = b * s
    xt = x.reshape(t, D_MODEL)
    uf, cq, ckv, z, xbc, small = _inproj(xt, p["pre_mix_g"], p["w_in"], _tile(t, 512))
    small3 = small.reshape(b, s, SMALL)

    fo = _fourier(uf.reshape(b, s, FNET_DIM)).reshape(t, FNET_DIM)

    q_t, k, v_t = _mla_prep(cq.reshape(b, s, Q_LORA), ckv.reshape(b, s, KV_LORA), small3,
                            p["q_norm_g"], p["kv_norm_g"], p["wq_t"], p["wk"], p["wv_t"],
                            _tile(s, 1024) if s >= 4096 else _tile(s, 512))
    mo = _flash(q_t, k, v_t, _tile(s, 512)).reshape(t, MLA_DIM)

    xs, bm, cm = _conv(xbc.reshape(b, s, CONV_DIM), p["conv_w"], p["conv_b"], _tile(s, 512))
    chunks = 4 if s % (4 * CHUNK) == 0 else 1
    yf, yb = _ssd(xs, bm, cm, small3, p["dt_bias_f"], p["dt_bias_b"], p["a_log_f"], p["a_log_b"],
                  chunks)

    x1 = _outproj(xt, fo, mo, yf.reshape(t, SSD_DIM), yb.reshape(t, SSD_DIM),
                  xs.reshape(t, SSD_DIM), z, p["d_skip"], p["ssd_norm_g"], p["w_out"],
                  p["post_mix_g"], _tile(t, 512))
    x2 = _ffn(x1, p["pre_ffn_g"], p["w_ff1"], p["w_ff2"], p["post_ffn_g"],
              _tile(t, 512), 1024)
    return x2.reshape(b, s, D_MODEL)


def kernel(x_prompt, x_sample, pre_mix_g, w_in, q_norm_g, w_q_up, kv_norm_g, w_kv_up, conv_w, conv_b, dt_bias_f, dt_bias_b, a_log_f, a_log_b, d_skip, ssd_norm_g, w_out, post_mix_g, pre_ffn_g, w_ff1, w_ff2, post_ffn_g):
    weights = (pre_mix_g, w_in, q_norm_g, w_q_up, kv_norm_g, w_kv_up, conv_w, conv_b,
               dt_bias_f, dt_bias_b, a_log_f, a_log_b, d_skip, ssd_norm_g, w_out,
               post_mix_g, pre_ffn_g, w_ff1, w_ff2, post_ffn_g)
    depth = w_in.shape[0]
    layers = [_prep_layer(*[w[l] for w in weights]) for l in range(depth)]

    def run_trunk(x):
        for p in layers:
            x = _layer(x, p)
        return x

    return (run_trunk(x_prompt), run_trunk(x_sample))
```

```python
import functools
import math

import jax
import jax.numpy as jnp
import numpy as np
from jax import lax
from jax.experimental import pallas as pl
from jax.experimental.pallas import tpu as pltpu

F32 = jnp.float32
BF16 = jnp.bfloat16

D_MODEL = 2048
FNET_GROUPS = 4
FNET_GROUP_DIM = 128
FNET_DIM = FNET_GROUPS * FNET_GROUP_DIM
MLA_HEADS = 6
Q_LORA = 512
KV_LORA = 512
QK_NOPE = 128
QK_ROPE = 64
ROPE_HALF = QK_ROPE // 2
V_HEAD = 128
V_AUG = V_HEAD + 16
QK_HEAD = QK_NOPE + QK_ROPE
QK_PAD = 256
MLA_DIM = MLA_HEADS * V_HEAD
ROPE_THETA = 10000.0
SSD_HEADS = 12
SSD_HEAD_DIM = 64
SSD_DIM = SSD_HEADS * SSD_HEAD_DIM
SSD_GROUPS = 2
SSD_HPG = SSD_HEADS // SSD_GROUPS
SSD_GROUP_DIM = SSD_DIM // SSD_GROUPS
SSD_STATE = 128
CONV_K = 5
CHUNK = 128
CONV_DIM = SSD_DIM + 2 * SSD_GROUPS * SSD_STATE
BC_DIM = SSD_GROUPS * SSD_STATE
D_FF = 4 * D_MODEL
EPS = 1e-6

LANES = 128
F32_ROWS = 8
BF16_ROWS = 16
SMALL = LANES
SMALL_DTF = ROPE_HALF
SMALL_DTB = ROPE_HALF + SSD_HEADS
SMALL_PE2 = LANES // 2
PROJ_DIM = FNET_DIM + Q_LORA + KV_LORA + SSD_DIM + CONV_DIM + SMALL

VMEM_LIMIT = 60 * 1024 * 1024
FNET_S2_TILE = 8
FNET_K1_TILE = 4


def _params(*sem):
    return pltpu.CompilerParams(dimension_semantics=sem, vmem_limit_bytes=VMEM_LIMIT)


def _dot(a, b):
    return jnp.dot(a, b, preferred_element_type=F32)


def _dot_nt(a, b):
    return lax.dot_general(a, b, (((1,), (1,)), ((), ())), preferred_element_type=F32)


def _dot_tn(a, b):
    return lax.dot_general(a, b, (((0,), (0,)), ((), ())), preferred_element_type=F32)


def _rms_scale(x):
    return lax.rsqrt(jnp.mean(x * x, axis=-1, keepdims=True) + EPS)


def _split3(x):
    hi = x.astype(BF16)
    r = x - hi.astype(F32)
    mid = r.astype(BF16)
    lo = (r - mid.astype(F32)).astype(BF16)
    return hi, mid, lo


def _dot_exact_rhs01(x, m01):
    hi, mid, lo = _split3(x)
    return _dot(hi, m01) + _dot(mid, m01) + _dot(lo, m01)


def _dot_exact_lhs01(m01, x):
    hi, mid, lo = _split3(x)
    return _dot(m01, hi) + _dot(m01, mid) + _dot(m01, lo)


def _resident(shape):
    nd = len(shape)
    return pl.BlockSpec(shape, lambda *_: (0,) * nd, pipeline_mode=pl.Buffered(1))


_IN_WIDTHS = (FNET_DIM, Q_LORA, KV_LORA, SSD_DIM, CONV_DIM, SMALL)


def _inproj_kernel(x_ref, g_ref, w_ref, uf_ref, cq_ref, ckv_ref, z_ref, xbc_ref, small_ref):
    x = x_ref[...]
    h = (x * _rms_scale(x) * g_ref[...]).astype(BF16)
    off = 0
    for ref, width in zip((uf_ref, cq_ref, ckv_ref, z_ref, xbc_ref, small_ref), _IN_WIDTHS):
        for c0 in range(0, width, 512):
            cw = min(512, width - c0)
            ref[:, c0:c0 + cw] = _dot(h, w_ref[:, off + c0:off + c0 + cw]).astype(ref.dtype)
        off += width


def _inproj(x, g, w, tm):
    t = x.shape[0]
    dtypes = (BF16, BF16, BF16, BF16, BF16, F32)
    return pl.pallas_call(
        _inproj_kernel,
        grid=(t // tm,),
        in_specs=[pl.BlockSpec((tm, D_MODEL), lambda i: (i, 0)),
                  _resident((1, D_MODEL)),
                  _resident((D_MODEL, PROJ_DIM))],
        out_specs=[pl.BlockSpec((tm, wd), lambda i: (i, 0)) for wd in _IN_WIDTHS],
        out_shape=[jax.ShapeDtypeStruct((t, wd), dt) for wd, dt in zip(_IN_WIDTHS, dtypes)],
        compiler_params=_params("parallel"),
    )(x, g, w)


def _dft_cos_sin(n):
    k = np.arange(n)
    ang = 2.0 * np.pi * ((k[:, None] * k[None, :]) % n) / n
    return np.cos(ang), np.sin(ang)


def _fnet_stage1_kernel(u_ref, m1_ref, twr_ref, twi_ref, yr_ref, yi_ref):
    n1 = u_ref.shape[0]
    y = _dot(m1_ref[...], u_ref[...])
    for s in range(FNET_S2_TILE):
        tr = twr_ref[:, s * LANES:(s + 1) * LANES]
        ti = twi_ref[:, s * LANES:(s + 1) * LANES]
        for q in range(FNET_DIM // LANES):
            c0 = s * FNET_DIM + q * LANES
            yr = y[:n1, c0:c0 + LANES]
            yi = y[n1:, c0:c0 + LANES]
            yr_ref[:, c0:c0 + LANES] = (yr * tr - yi * ti).astype(BF16)
            yi_ref[:, c0:c0 + LANES] = (yr * ti + yi * tr).astype(BF16)


def _fnet_stage2_kernel(yr_ref, yi_ref, m2_ref, mc_ref, o_ref):
    n2 = yr_ref.shape[1]
    for k1 in range(yr_ref.shape[0]):
        y = jnp.concatenate([yr_ref[k1], yi_ref[k1]], axis=0)
        h = _dot(m2_ref[...], y).astype(BF16)
        for g in range(FNET_GROUPS):
            c0 = g * FNET_GROUP_DIM
            hg = jnp.concatenate([h[:n2, c0:c0 + FNET_GROUP_DIM], h[n2:, c0:c0 + FNET_GROUP_DIM]],
                                 axis=1)
            o_ref[k1, :, c0:c0 + FNET_GROUP_DIM] = _dot(hg, mc_ref[...]).astype(BF16)


def _fourier(u):
    b, s, c = u.shape
    n2 = LANES
    n1 = s // n2
    assert n1 * n2 == s and n1 % 16 == 0 and n2 % FNET_S2_TILE == 0
    c1, s1 = _dft_cos_sin(n1)
    m1 = jnp.asarray(np.concatenate([c1, -s1], axis=0), BF16)
    c2, s2 = _dft_cos_sin(n2)
    m2 = jnp.asarray(np.block([[c2, s2], [-s2, c2]]), BF16)
    cc, sc = _dft_cos_sin(FNET_GROUP_DIM)
    mc = jnp.asarray(np.concatenate([cc, sc], axis=0), BF16)
    ang = 2.0 * np.pi * (np.arange(n1)[:, None] * np.arange(n2)[None, :]) / s
    scale = 1.0 / math.sqrt(s * FNET_GROUP_DIM)
    twr = jnp.repeat(jnp.asarray(np.cos(ang) * scale, F32), LANES, axis=1)
    twi = jnp.repeat(jnp.asarray(-np.sin(ang) * scale, F32), LANES, axis=1)

    cols = FNET_S2_TILE * c
    u2 = u.reshape(b, n1, n2 * c)
    yr, yi = pl.pallas_call(
        _fnet_stage1_kernel,
        grid=(b, n2 // FNET_S2_TILE),
        in_specs=[pl.BlockSpec((None, n1, cols), lambda i, j: (i, 0, j)),
                  _resident((2 * n1, n1)),
                  pl.BlockSpec((n1, FNET_S2_TILE * LANES), lambda i, j: (0, j)),
                  pl.BlockSpec((n1, FNET_S2_TILE * LANES), lambda i, j: (0, j))],
        out_specs=[pl.BlockSpec((None, n1, cols), lambda i, j: (i, 0, j))] * 2,
        out_shape=[jax.ShapeDtypeStruct((b, n1, n2 * c), BF16)] * 2,
        compiler_params=_params("parallel", "parallel"),
    )(u2, m1, twr, twi)

    yr = yr.reshape(b, n1, n2, c)
    yi = yi.reshape(b, n1, n2, c)
    out_t = pl.pallas_call(
        _fnet_stage2_kernel,
        grid=(b, n1 // FNET_K1_TILE),
        in_specs=[pl.BlockSpec((None, FNET_K1_TILE, n2, c), lambda i, j: (i, j, 0, 0)),
                  pl.BlockSpec((None, FNET_K1_TILE, n2, c), lambda i, j: (i, j, 0, 0)),
                  _resident((2 * n2, 2 * n2)),
                  _resident((2 * FNET_GROUP_DIM, FNET_GROUP_DIM))],
        out_specs=pl.BlockSpec((None, FNET_K1_TILE, n2, c), lambda i, j: (i, j, 0, 0)),
        out_shape=jax.ShapeDtypeStruct((b, n1, n2, c), BF16),
        compiler_params=_params("parallel", "parallel"),
    )(yr, yi, m2, mc)
    return out_t.transpose(0, 2, 1, 3).reshape(b, s, c)


def _mla_prep_kernel(cq_ref, ckv_ref, small_ref, cos_t_ref, sin_t_ref, cos_k_ref, sin_k_ref,
                     qg_ref, kvg_ref, wq_t_ref, wk_ref, wv_t_ref, q_t_ref, k_ref, v_t_ref):
    cq = cq_ref[...].astype(F32)
    cqn = (cq * _rms_scale(cq) * qg_ref[...]).astype(BF16)
    ckv = ckv_ref[...].astype(F32)
    ckvn = (ckv * _rms_scale(ckv) * kvg_ref[...]).astype(BF16)
    q_scale = math.log2(math.e) / math.sqrt(QK_HEAD)
    cos_t = cos_t_ref[...]
    sin_t = sin_t_ref[...]
    half = LANES // 2
    for h in range(MLA_HEADS):
        r0 = h * QK_PAD
        q_t = _dot_nt(wq_t_ref[r0:r0 + QK_PAD, :], cqn)
        q_t_ref[r0:r0 + QK_NOPE, :] = (q_t[:QK_NOPE] * q_scale).astype(BF16)
        blk = q_t[QK_NOPE:]
        swapped = jnp.concatenate([blk[half:], blk[:half]], axis=0)
        q_t_ref[r0 + QK_NOPE:r0 + QK_PAD, :] = ((blk * cos_t + swapped * sin_t) * q_scale).astype(BF16)

    small = small_ref[...]
    k_rope = (small * cos_k_ref[...] + pltpu.roll(small, half, axis=1) * sin_k_ref[...]).astype(BF16)
    for h in range(MLA_HEADS):
        c0 = h * QK_PAD
        k_ref[:, c0:c0 + QK_NOPE] = _dot(ckvn, wk_ref[:, h * QK_NOPE:(h + 1) * QK_NOPE]).astype(BF16)
        k_ref[:, c0 + QK_NOPE:c0 + QK_PAD] = k_rope
        v_t_ref[h, :V_HEAD, :] = _dot_nt(wv_t_ref[h * V_HEAD:(h + 1) * V_HEAD, :], ckvn).astype(BF16)
        v_t_ref[h, V_HEAD:, :] = jnp.ones((V_AUG - V_HEAD, v_t_ref.shape[2]), BF16)


def _rope_tables(s):
    pos = jnp.arange(s, dtype=F32)
    inv = ROPE_THETA ** (-jnp.arange(0, QK_ROPE, 2, dtype=F32) / QK_ROPE)
    ang = pos[:, None] * inv[None, :]
    cos, sin = jnp.cos(ang), jnp.sin(ang)
    zero = jnp.zeros_like(cos)
    cos_k = jnp.concatenate([cos, zero, cos, zero], axis=1)
    sin_k = jnp.concatenate([-sin, zero, sin, zero], axis=1)
    return cos_k, sin_k


def _mla_prep(cq, ckv, small, qg, kvg, wq_t, wk, wv_t, tm):
    b, s, _ = cq.shape
    cos_k, sin_k = _rope_tables(s)
    cos_t, sin_t = cos_k.T, sin_k.T
    tok = lambda i, j: (i, j, 0)
    tok_t = lambda i, j: (i, 0, j)
    return pl.pallas_call(
        _mla_prep_kernel,
        grid=(b, s // tm),
        in_specs=[pl.BlockSpec((None, tm, Q_LORA), tok),
                  pl.BlockSpec((None, tm, KV_LORA), tok),
                  pl.BlockSpec((None, tm, SMALL), tok),
                  pl.BlockSpec((LANES, tm), lambda i, j: (0, j)),
                  pl.BlockSpec((LANES, tm), lambda i, j: (0, j)),
                  pl.BlockSpec((tm, LANES), lambda i, j: (j, 0)),
                  pl.BlockSpec((tm, LANES), lambda i, j: (j, 0)),
                  _resident((1, Q_LORA)),
                  _resident((1, KV_LORA)),
                  _resident((MLA_HEADS * QK_PAD, Q_LORA)),
                  _resident((KV_LORA, MLA_HEADS * QK_NOPE)),
                  _resident((MLA_DIM, KV_LORA))],
        out_specs=[pl.BlockSpec((None, MLA_HEADS * QK_PAD, tm), tok_t),
                   pl.BlockSpec((None, tm, MLA_HEADS * QK_PAD), tok),
                   pl.BlockSpec((None, MLA_HEADS, None, V_AUG, tm), lambda i, j: (i, 0, j, 0, 0))],
        out_shape=[jax.ShapeDtypeStruct((b, MLA_HEADS * QK_PAD, s), BF16),
                   jax.ShapeDtypeStruct((b, s, MLA_HEADS * QK_PAD), BF16),
                   jax.ShapeDtypeStruct((b, MLA_HEADS, s // tm, V_AUG, tm), BF16)],
        compiler_params=_params("parallel", "parallel"),
    )(cq, ckv, small, cos_t, sin_t, cos_k, sin_k, qg, kvg, wq_t, wk, wv_t)


FLASH_RING = 4
FLASH_PASS = 4


def _flash_kernel(q_t_ref, k_ref, v_t_ref, o_ref, s0_ref, s1_ref, s2_ref, s3_ref, p0_ref, p1_ref,
                  m_ref, acc_ref):
    n, _, sub = v_t_ref.shape
    tq = q_t_ref.shape[1]
    s_refs = (s0_ref, s1_ref, s2_ref, s3_ref)
    p_refs = (p0_ref, p1_ref)
    m_ref[...] = jnp.full(m_ref.shape, -jnp.inf, F32)
    acc_ref[...] = jnp.zeros(acc_ref.shape, F32)

    def scores(c):
        k0 = c * sub if isinstance(c, int) else pl.multiple_of(c * sub, sub)
        return _dot(k_ref[pl.ds(k0, sub), :], q_t_ref[...])

    def softmax(s_ref, p_ref):
        m8 = s_ref[0:F32_ROWS, :]
        for r in range(F32_ROWS, sub, F32_ROWS):
            m8 = jnp.maximum(m8, s_ref[r:r + F32_ROWS, :])
        m_prev = m_ref[...]
        m_new = jnp.maximum(m_prev, jnp.max(m8, axis=0, keepdims=True))
        alpha = jnp.exp2(m_prev - m_new)
        m_rows = jnp.broadcast_to(m_new, (BF16_ROWS, tq))
        for r in range(0, sub, BF16_ROWS):
            p_ref[r:r + BF16_ROWS, :] = jnp.exp2((s_ref[r:r + BF16_ROWS, :] - m_rows).astype(BF16))
        m_ref[...] = m_new
        return alpha

    blocks = min(FLASH_PASS, n)

    def ring_pass(c, lookahead, first):
        for j in range(blocks):
            if j < lookahead:
                s_refs[(j + 2) % FLASH_RING][...] = scores(c + j + 2)
            if first and j == 0:
                softmax(s_refs[0], p_refs[0])
                continue
            pv = _dot(v_t_ref[c + j - 1], p_refs[(j + 1) % 2][...])
            alpha = softmax(s_refs[j % FLASH_RING], p_refs[j % 2])
            acc_ref[...] = (acc_ref[...] + pv) * alpha

    s0_ref[...] = scores(0)
    s1_ref[...] = scores(1)
    last = n - blocks
    if last == 0:
        ring_pass(0, blocks - 2, True)
    else:
        ring_pass(0, blocks, True)

        def body(i, carry):
            ring_pass(i * blocks, blocks, False)
            return carry

        lax.fori_loop(1, last // blocks, body, 0)
        ring_pass(last, blocks - 2, False)
    acc = acc_ref[...] + _dot(v_t_ref[n - 1], p_refs[(n - 1) % 2][...])
    o_ref[...] = (acc[:V_HEAD] / acc[V_HEAD:V_HEAD + 1]).T.astype(o_ref.dtype)


def _flash(q_t, k, v_t, tq):
    b, s, _ = k.shape
    n, sub = v_t.shape[2], v_t.shape[4]
    assert n % FLASH_RING == 0 and n % min(FLASH_PASS, n) == 0 and n * sub == s
    return pl.pallas_call(
        _flash_kernel,
        grid=(b, MLA_HEADS, s // tq),
        in_specs=[pl.BlockSpec((None, QK_PAD, tq), lambda i, h, qi: (i, h, qi)),
                  pl.BlockSpec((None, s, QK_PAD), lambda i, h, qi: (i, 0, h)),
                  pl.BlockSpec((None, None, n, V_AUG, sub), lambda i, h, qi: (i, h, 0, 0, 0))],
        out_specs=pl.BlockSpec((None, tq, V_HEAD), lambda i, h, qi: (i, qi, h)),
        out_shape=jax.ShapeDtypeStruct((b, s, MLA_DIM), BF16),
        scratch_shapes=[pltpu.VMEM((sub, tq), F32)] * FLASH_RING
                       + [pltpu.VMEM((sub, tq), BF16)] * 2
                       + [pltpu.VMEM((1, tq), F32), pltpu.VMEM((V_AUG, tq), F32)],
        compiler_params=_params("parallel", "parallel", "arbitrary"),
    )(q_t, k, v_t)


CONV_HALO = 16
CONV_PAD = 8


def _conv_kernel(cur_ref, prev_ref, next_ref, w_ref, b_ref, xs_ref, bm_ref, cm_ref, ext_ref):
    i = pl.program_id(1)
    tc = cur_ref.shape[0]
    prev = prev_ref[...].astype(F32)[CONV_HALO - CONV_PAD:]
    nxt = next_ref[...].astype(F32)[:CONV_PAD]
    ext_ref[0:CONV_PAD, :] = jnp.where(i > 0, prev, 0.0)
    ext_ref[CONV_PAD:CONV_PAD + tc, :] = cur_ref[...].astype(F32)
    ext_ref[CONV_PAD + tc:, :] = jnp.where(i < pl.num_programs(1) - 1, nxt, 0.0)
    acc = jnp.broadcast_to(b_ref[...], (tc, CONV_DIM))
    for k in range(CONV_K):
        acc = acc + w_ref[k:k + 1, :] * ext_ref[pl.ds(CONV_PAD - CONV_K // 2 + k, tc), :]
    y = acc / (1.0 + jnp.exp(-acc))
    xs_ref[...] = y[:, :SSD_DIM].astype(BF16)
    bm_ref[...] = y[:, SSD_DIM:SSD_DIM + BC_DIM].astype(BF16)
    cm_ref[...] = y[:, SSD_DIM + BC_DIM:].astype(BF16)


def _conv(xbc, w, bias, tc):
    b, s, _ = xbc.shape
    per = tc // CONV_HALO
    last = s // CONV_HALO - 1
    return pl.pallas_call(
        _conv_kernel,
        grid=(b, s // tc),
        in_specs=[pl.BlockSpec((None, tc, CONV_DIM), lambda i, j: (i, j, 0)),
                  pl.BlockSpec((None, CONV_HALO, CONV_DIM),
                               lambda i, j: (i, jnp.maximum(j * per - 1, 0), 0)),
                  pl.BlockSpec((None, CONV_HALO, CONV_DIM),
                               lambda i, j: (i, jnp.minimum((j + 1) * per, last), 0)),
                  _resident((8, CONV_DIM)),
                  _resident((1, CONV_DIM))],
        out_specs=[pl.BlockSpec((None, tc, SSD_DIM), lambda i, j: (i, j, 0)),
                   pl.BlockSpec((None, tc, BC_DIM), lambda i, j: (i, j, 0)),
                   pl.BlockSpec((None, tc, BC_DIM), lambda i, j: (i, j, 0))],
        out_shape=[jax.ShapeDtypeStruct((b, s, SSD_DIM), BF16),
                   jax.ShapeDtypeStruct((b, s, BC_DIM), BF16),
                   jax.ShapeDtypeStruct((b, s, BC_DIM), BF16)],
        scratch_shapes=[pltpu.VMEM((tc + 2 * CONV_PAD, CONV_DIM), F32)],
        compiler_params=_params("parallel", "parallel"),
    )(xbc, xbc, xbc, w, bias)


class _SsdChain:
    def __init__(self, direction, r0, xs_ref, bm_ref, cm_ref, small_ref, y_ref):
        self.d, self.r0 = direction, r0
        self.rows = slice(r0, r0 + CHUNK)
        self.xs_ref, self.bm_ref, self.cm_ref, self.small_ref, self.y_ref = (
            xs_ref, bm_ref, cm_ref, small_ref, y_ref)

    def step_size(self, sel_ref, bias_ref):
        raw = _dot_exact_rhs01(self.small_ref[self.rows, :], sel_ref[self.d]) + bias_ref[self.d]
        self.dt = jnp.maximum(raw, 0.0) + jnp.log1p(jnp.exp(-jnp.abs(raw)))

    def decay_log(self, tri_ref, a_heads):
        self.g = _dot_exact_lhs01(tri_ref[self.d], self.dt * a_heads[self.d])

    def widen(self, e64_ref):
        self.g_t = self.g.T
        self.g_wide = _dot_exact_rhs01(self.g, e64_ref[...])
        dt_hi = self.dt.astype(BF16)
        dt_lo = (self.dt - dt_hi.astype(F32)).astype(BF16)
        self.dt_wide = _dot(dt_hi, e64_ref[...]) + _dot(dt_lo, e64_ref[...])

    def within_chunk(self):
        reverse = self.d == 1
        row = lax.broadcasted_iota(jnp.int32, (CHUNK, CHUNK), 0)
        col = lax.broadcasted_iota(jnp.int32, (CHUNK, CHUNK), 1)
        mask = (row <= col) if reverse else (row >= col)
        end = 0 if reverse else CHUNK - 1
        g, g_t, g_wide = self.g, self.g_t, self.g_wide
        self.g_end = g_wide[end:end + 1, :]
        xd = self.xs_ref[self.rows, :].astype(F32) * self.dt_wide
        first = lax.broadcasted_iota(jnp.int32, (CHUNK, SSD_DIM), 1) % LANES < SSD_HEAD_DIM
        xd_first = jnp.where(first, xd, 0.0).astype(BF16)
        xd_second = jnp.where(first, 0.0, xd).astype(BF16)
        self.xdw = (xd * jnp.exp(self.g_end - g_wide)).astype(BF16)
        self.scale_off = jnp.exp(g_wide)
        bm = self.bm_ref[self.rows, :]
        cm = self.cm_ref[self.rows, :]
        self.b_g, self.c_g, self.y_diag = [], [], []
        for grp in range(SSD_GROUPS):
            n0 = grp * SSD_STATE
            b_g = bm[:, n0:n0 + SSD_STATE]
            c_g = cm[:, n0:n0 + SSD_STATE]
            self.b_g.append(b_g)
            self.c_g.append(c_g)
            cb = _dot_nt(c_g, b_g)
            for pr in range(SSD_HPG // 2):
                h0 = grp * SSD_HPG + 2 * pr
                ws = []
                for h in (h0, h0 + 1):
                    diff = g[:, h:h + 1] - g_t[h:h + 1, :]
                    ws.append((cb * jnp.exp(jnp.where(mask, diff, -jnp.inf))).astype(BF16))
                p0 = h0 * SSD_HEAD_DIM
                rhs = jnp.concatenate([xd_first[:, p0:p0 + LANES], xd_second[:, p0:p0 + LANES]],
                                      axis=0)
                self.y_diag.append(_dot(jnp.concatenate(ws, axis=1), rhs))

    def across_chunks(self, state_ref):
        for grp in range(SSD_GROUPS):
            d0 = grp * SSD_GROUP_DIM
            state = state_ref[self.d, grp]
            y_off = (_dot(self.c_g[grp], state.astype(BF16))
                     * self.scale_off[:, d0:d0 + SSD_GROUP_DIM])
            for pr in range(SSD_HPG // 2):
                p0 = d0 + pr * LANES
                y = self.y_diag[grp * (SSD_HPG // 2) + pr] + y_off[:, pr * LANES:(pr + 1) * LANES]
                self.y_ref[self.rows, p0:p0 + LANES] = y.astype(self.y_ref.dtype)
            state_ref[self.d, grp] = (state * jnp.exp(self.g_end[:, d0:d0 + SSD_GROUP_DIM])
                                      + _dot_tn(self.b_g[grp], self.xdw[:, d0:d0 + SSD_GROUP_DIM]))


def _ssd_kernel(xs_f_ref, bm_f_ref, cm_f_ref, small_f_ref, xs_b_ref, bm_b_ref, cm_b_ref, small_b_ref,
                sel_ref, e64_ref, tri_ref, bias_ref, alog_ref, yf_ref, yb_ref, state_ref, *, chunks):
    @pl.when(pl.program_id(1) == 0)
    def _():
        state_ref[...] = jnp.zeros(state_ref.shape, F32)

    a_heads = -jnp.exp(alog_ref[...])
    chains = []
    for c in range(chunks):
        chains.append(_SsdChain(0, c * CHUNK, xs_f_ref, bm_f_ref, cm_f_ref, small_f_ref, yf_ref))
        chains.append(_SsdChain(1, (chunks - 1 - c) * CHUNK, xs_b_ref, bm_b_ref, cm_b_ref,
                                small_b_ref, yb_ref))
    for ch in chains:
        ch.step_size(sel_ref, bias_ref)
    for ch in chains:
        ch.decay_log(tri_ref, a_heads)
    for ch in chains:
        ch.widen(e64_ref)
    for ch in chains:
        ch.within_chunk()
    for ch in chains:
        ch.across_chunks(state_ref)


def _ssd_consts():
    sel = np.zeros((2, LANES, LANES), np.float32)
    for d, base in enumerate((SMALL_DTF, SMALL_DTB)):
        sel[d, base + np.arange(SSD_HEADS), np.arange(SSD_HEADS)] = 1.0
    e64 = np.zeros((LANES, SSD_DIM), np.float32)
    for h in range(SSD_HEADS):
        e64[h, h * SSD_HEAD_DIM:(h + 1) * SSD_HEAD_DIM] = 1.0
    lower = np.tril(np.ones((CHUNK, CHUNK), np.float32))
    tri = np.stack([lower, lower.T])
    return jnp.asarray(sel, BF16), jnp.asarray(e64, BF16), jnp.asarray(tri, BF16)


def _ssd(xs, bm, cm, small, dt_bias_f, dt_bias_b, a_log_f, a_log_b, chunks):
    b, s, _ = xs.shape
    rows = chunks * CHUNK
    n = s // rows
    sel, e64, tri = _ssd_consts()
    pad = lambda f, bk: jnp.pad(jnp.stack([f, bk]).astype(F32),
                                ((0, 0), (0, LANES - SSD_HEADS))).reshape(2, 1, LANES)
    fwd = lambda i, j: (i, j, 0)
    bwd = lambda i, j: (i, n - 1 - j, 0)
    tiles = lambda tok: [pl.BlockSpec((None, rows, SSD_DIM), tok),
                         pl.BlockSpec((None, rows, BC_DIM), tok),
                         pl.BlockSpec((None, rows, BC_DIM), tok),
                         pl.BlockSpec((None, rows, SMALL), tok)]
    return pl.pallas_call(
        functools.partial(_ssd_kernel, chunks=chunks),
        grid=(b, n),
        in_specs=tiles(fwd) + tiles(bwd) + [
            _resident((2, LANES, LANES)),
            _resident((LANES, SSD_DIM)),
            _resident((2, CHUNK, CHUNK)),
            _resident((2, 1, LANES)),
            _resident((2, 1, LANES))],
        out_specs=[pl.BlockSpec((None, rows, SSD_DIM), fwd), pl.BlockSpec((None, rows, SSD_DIM), bwd)],
        out_shape=[jax.ShapeDtypeStruct((b, s, SSD_DIM), BF16)] * 2,
        scratch_shapes=[pltpu.VMEM((2, SSD_GROUPS, SSD_STATE, SSD_GROUP_DIM), F32)],
        compiler_params=_params("parallel", "arbitrary"),
    )(xs, bm, cm, small, xs, bm, cm, small, sel, e64, tri,
      pad(dt_bias_f, dt_bias_b), pad(a_log_f, a_log_b))


def _outproj_kernel(x_ref, fo_ref, mo_ref, yf_ref, yb_ref, xs_ref, z_ref, dskip_ref, ng_ref,
                    w_ref, pg_ref, o_ref):
    y = (yf_ref[...].astype(F32) + yb_ref[...].astype(F32)
         + xs_ref[...].astype(F32) * dskip_ref[...])
    z = z_ref[...].astype(F32)
    y = y * (z / (1.0 + jnp.exp(-z)))
    parts = []
    for grp in range(SSD_GROUPS):
        yg = y[:, grp * SSD_GROUP_DIM:(grp + 1) * SSD_GROUP_DIM]
        parts.append(yg * _rms_scale(yg))
    y = (jnp.concatenate(parts, axis=1) * ng_ref[...]).astype(BF16)
    mix = (_dot(fo_ref[...], w_ref[0:FNET_DIM, :])
           + _dot(mo_ref[...], w_ref[FNET_DIM:FNET_DIM + MLA_DIM, :])
           + _dot(y, w_ref[FNET_DIM + MLA_DIM:, :]))
    o_ref[...] = x_ref[...] + mix * _rms_scale(mix) * pg_ref[...]


def _outproj(x, fo, mo, yf, yb, xs, z, dskip, ng, w, pg, tm):
    t = x.shape[0]
    rows = lambda wd: pl.BlockSpec((tm, wd), lambda i: (i, 0))
    return pl.pallas_call(
        _outproj_kernel,
        grid=(t // tm,),
        in_specs=[rows(D_MODEL), rows(FNET_DIM), rows(MLA_DIM), rows(SSD_DIM), rows(SSD_DIM),
                  rows(SSD_DIM), rows(SSD_DIM), _resident((1, SSD_DIM)), _resident((1, SSD_DIM)),
                  _resident((D_MODEL, D_MODEL)), _resident((1, D_MODEL))],
        out_specs=rows(D_MODEL),
        out_shape=jax.ShapeDtypeStruct((t, D_MODEL), F32),
        compiler_params=_params("parallel"),
    )(x, fo, mo, yf, yb, xs, z, dskip, ng, w, pg)


FFN_SUB = 1024


def _ffn_kernel(x_ref, g_ref, w1_ref, w2_ref, pg_ref, o_ref, h_ref):
    j = pl.program_id(1)

    @pl.when(j == 0)
    def _():
        x = x_ref[...]
        h_ref[...] = (x * _rms_scale(x) * g_ref[...]).astype(BF16)
        o_ref[...] = jnp.zeros(o_ref.shape, F32)

    for c0 in range(0, w1_ref.shape[1], FFN_SUB):
        a = jnp.maximum(_dot(h_ref[...], w1_ref[:, c0:c0 + FFN_SUB]), 0.0)
        o_ref[...] += _dot((a * a).astype(BF16), w2_ref[c0:c0 + FFN_SUB, :])

    @pl.when(j == pl.num_programs(1) - 1)
    def _():
        f = o_ref[...]
        o_ref[...] = x_ref[...] + f * _rms_scale(f) * pg_ref[...]


def _ffn(x, g, w1, w2, pg, layer, tm, tf):
    t = x.shape[0]
    return pl.pallas_call(
        _ffn_kernel,
        grid=(t // tm, D_FF // tf),
        in_specs=[pl.BlockSpec((tm, D_MODEL), lambda i, j: (i, 0)),
                  _resident((1, D_MODEL)),
                  pl.BlockSpec((None, D_MODEL, tf), lambda i, j: (layer, 0, j)),
                  pl.BlockSpec((None, tf, D_MODEL), lambda i, j: (layer, j, 0)),
                  _resident((1, D_MODEL))],
        out_specs=pl.BlockSpec((tm, D_MODEL), lambda i, j: (i, 0)),
        out_shape=jax.ShapeDtypeStruct((t, D_MODEL), F32),
        scratch_shapes=[pltpu.VMEM((tm, D_MODEL), BF16)],
        compiler_params=_params("parallel", "arbitrary"),
    )(x, g, w1, w2, pg)


def _prep_layer(pre_mix_g, w_in, q_norm_g, w_q_up, kv_norm_g, w_kv_up, conv_w, conv_b,
                dt_bias_f, dt_bias_b, a_log_f, a_log_b, d_skip, ssd_norm_g, w_out,
                post_mix_g, pre_ffn_g, w_ff1, w_ff2, post_ffn_g):
    o_cq = FNET_DIM
    o_ckv = o_cq + Q_LORA
    o_pe = o_ckv + KV_LORA
    o_z = o_pe + QK_ROPE
    o_xbc = o_z + SSD_DIM
    o_dt = o_xbc + CONV_DIM
    zeros = lambda n: jnp.zeros((D_MODEL, n), w_in.dtype)
    small_cols = jnp.concatenate([
        w_in[:, o_pe:o_pe + ROPE_HALF],
        w_in[:, o_dt:o_dt + 2 * SSD_HEADS],
        zeros(SMALL_PE2 - ROPE_HALF - 2 * SSD_HEADS),
        w_in[:, o_pe + ROPE_HALF:o_pe + QK_ROPE],
        zeros(LANES - SMALL_PE2 - ROPE_HALF)], axis=1)
    w_in_r = jnp.concatenate([w_in[:, :o_pe], w_in[:, o_z:o_dt], small_cols], axis=1).astype(BF16)

    wq = w_q_up.reshape(Q_LORA, MLA_HEADS, QK_HEAD)
    zq = jnp.zeros((Q_LORA, MLA_HEADS, ROPE_HALF), w_q_up.dtype)
    wq = jnp.concatenate([wq[..., :QK_NOPE + ROPE_HALF], zq, wq[..., QK_NOPE + ROPE_HALF:], zq], axis=-1)
    wq_t = wq.reshape(Q_LORA, MLA_HEADS * QK_PAD).T.astype(BF16)
    wkv = w_kv_up.reshape(KV_LORA, MLA_HEADS, QK_NOPE + V_HEAD)
    wk = wkv[..., :QK_NOPE].reshape(KV_LORA, MLA_HEADS * QK_NOPE).astype(BF16)
    wv_t = wkv[..., QK_NOPE:].reshape(KV_LORA, MLA_DIM).T.astype(BF16)

    row = lambda v: v.astype(F32).reshape(1, -1)
    return dict(
        pre_mix_g=row(pre_mix_g), w_in=w_in_r, q_norm_g=row(q_norm_g), kv_norm_g=row(kv_norm_g),
        wq_t=wq_t, wk=wk, wv_t=wv_t,
        conv_w=jnp.pad(conv_w.astype(F32), ((0, 8 - CONV_K), (0, 0))), conv_b=row(conv_b),
        dt_bias_f=dt_bias_f, dt_bias_b=dt_bias_b, a_log_f=a_log_f, a_log_b=a_log_b,
        d_skip=row(jnp.repeat(d_skip, SSD_HEAD_DIM)), ssd_norm_g=row(ssd_norm_g),
        w_out=w_out.astype(BF16), post_mix_g=row(post_mix_g), pre_ffn_g=row(pre_ffn_g),
        post_ffn_g=row(post_ffn_g))


def _tile(n, want):
    return want if n % want == 0 else n


def _layer(x, p):
    b, s, _ = x.shape
    t = b * s
    xt = x.reshape(t, D_MODEL)
    uf, cq, ckv, z, xbc, small = _inproj(xt, p["pre_mix_g"], p["w_in"], _tile(t, 512))
    small3 = small.reshape(b, s, SMALL)

    fo = _fourier(uf.reshape(b, s, FNET_DIM)).reshape(t, FNET_DIM)

    q_t, k, v_t = _mla_prep(cq.reshape(b, s, Q_LORA), ckv.reshape(b, s, KV_LORA), small3,
                            p["q_norm_g"], p["kv_norm_g"], p["wq_t"], p["wk"], p["wv_t"],
                            _tile(s, 1024) if s >= 4096 else _tile(s, 512))
    mo = _flash(q_t, k, v_t, _tile(s, 512) if s >= 4096 else _tile(s, 1024)).reshape(t, MLA_DIM)

    xs, bm, cm = _conv(xbc.reshape(b, s, CONV_DIM), p["conv_w"], p["conv_b"], _tile(s, 512))
    chunks = 4 if s % (4 * CHUNK) == 0 else 1
    yf, yb = _ssd(xs, bm, cm, small3, p["dt_bias_f"], p["dt_bias_b"], p["a_log_f"], p["a_log_b"],
                  chunks)

    x1 = _outproj(xt, fo, mo, yf.reshape(t, SSD_DIM), yb.reshape(t, SSD_DIM),
                  xs.reshape(t, SSD_DIM), z, p["d_skip"], p["ssd_norm_g"], p["w_out"],
                  p["post_mix_g"], _tile(t, 512))
    x2 = _ffn(x1, p["pre_ffn_g"], p["w_ff1"], p["w_ff2"], p["post_ffn_g"], p["layer"],
              _tile(t, 512), 2048)
    return x2.reshape(b, s, D_MODEL)


def kernel(x_prompt, x_sample, pre_mix_g, w_in, q_norm_g, w_q_up, kv_norm_g, w_kv_up, conv_w, conv_b, dt_bias_f, dt_bias_b, a_log_f, a_log_b, d_skip, ssd_norm_g, w_out, post_mix_g, pre_ffn_g, w_ff1, w_ff2, post_ffn_g):
    weights = (pre_mix_g, w_in, q_norm_g, w_q_up, kv_norm_g, w_kv_up, conv_w, conv_b,
               dt_bias_f, dt_bias_b, a_log_f, a_log_b, d_skip, ssd_norm_g, w_out,
               post_mix_g, pre_ffn_g, w_ff1, w_ff2, post_ffn_g)
    depth = w_in.shape[0]
    layers = [_prep_layer(*[w[l] for w in weights]) for l in range(depth)]
    w_ff1_b, w_ff2_b = w_ff1.astype(BF16), w_ff2.astype(BF16)
    for l, p in enumerate(layers):
        p.update(layer=l, w_ff1=w_ff1_b, w_ff2=w_ff2_b)

    def run_trunk(x):
        for p in layers:
            x = _layer(x, p)
        return x

    return (run_trunk(x_prompt), run_trunk(x_sample))
```

```python
import functools
import math

import jax
import jax.numpy as jnp
import numpy as np
from jax import lax
from jax.experimental import pallas as pl
from jax.experimental.pallas import tpu as pltpu

F32 = jnp.float32
BF16 = jnp.bfloat16

D_MODEL = 2048
FNET_GROUPS = 4
FNET_GROUP_DIM = 128
FNET_DIM = FNET_GROUPS * FNET_GROUP_DIM
MLA_HEADS = 6
Q_LORA = 512
KV_LORA = 512
QK_NOPE = 128
QK_ROPE = 64
ROPE_HALF = QK_ROPE // 2
V_HEAD = 128
V_AUG = V_HEAD + 16
QK_HEAD = QK_NOPE + QK_ROPE
QK_PAD = 256
MLA_DIM = MLA_HEADS * V_HEAD
ROPE_THETA = 10000.0
SSD_HEADS = 12
SSD_HEAD_DIM = 64
SSD_DIM = SSD_HEADS * SSD_HEAD_DIM
SSD_GROUPS = 2
SSD_HPG = SSD_HEADS // SSD_GROUPS
SSD_GROUP_DIM = SSD_DIM // SSD_GROUPS
SSD_STATE = 128
CONV_K = 5
CHUNK = 128
CONV_DIM = SSD_DIM + 2 * SSD_GROUPS * SSD_STATE
BC_DIM = SSD_GROUPS * SSD_STATE
D_FF = 4 * D_MODEL
EPS = 1e-6

LANES = 128
F32_ROWS = 8
BF16_ROWS = 16
SMALL = LANES
SMALL_DTF = ROPE_HALF
SMALL_DTB = ROPE_HALF + SSD_HEADS
SMALL_PE2 = LANES // 2
PROJ_DIM = FNET_DIM + Q_LORA + KV_LORA + SSD_DIM + CONV_DIM + SMALL

VMEM_LIMIT = 60 * 1024 * 1024
FNET_S2_TILE = 8
FNET_K1_TILE = 8


def _params(*sem):
    return pltpu.CompilerParams(dimension_semantics=sem, vmem_limit_bytes=VMEM_LIMIT)


def _dot(a, b):
    return jnp.dot(a, b, preferred_element_type=F32)


def _dot_nt(a, b):
    return lax.dot_general(a, b, (((1,), (1,)), ((), ())), preferred_element_type=F32)


def _dot_tn(a, b):
    return lax.dot_general(a, b, (((0,), (0,)), ((), ())), preferred_element_type=F32)


def _rms_scale(x):
    return lax.rsqrt(jnp.mean(x * x, axis=-1, keepdims=True) + EPS)


def _split3(x):
    hi = x.astype(BF16)
    r = x - hi.astype(F32)
    mid = r.astype(BF16)
    lo = (r - mid.astype(F32)).astype(BF16)
    return hi, mid, lo


def _dot_exact_rhs01(x, m01):
    hi, mid, lo = _split3(x)
    return _dot(hi, m01) + _dot(mid, m01) + _dot(lo, m01)


def _split2(x):
    hi = x.astype(BF16)
    return hi, (x - hi.astype(F32)).astype(BF16)


def _dot_split2_rhs01(x, m01):
    hi, lo = _split2(x)
    return _dot(hi, m01) + _dot(lo, m01)


def _dot_split2_lhs01(m01, x):
    hi, lo = _split2(x)
    return _dot(m01, hi) + _dot(m01, lo)


def _resident(shape):
    nd = len(shape)
    return pl.BlockSpec(shape, lambda *_: (0,) * nd, pipeline_mode=pl.Buffered(1))


_IN_WIDTHS = (FNET_DIM, Q_LORA, KV_LORA, SSD_DIM, CONV_DIM, SMALL)


def _inproj_kernel(x_ref, g_ref, w_ref, uf_ref, cq_ref, ckv_ref, z_ref, xbc_ref, small_ref):
    x = x_ref[...]
    h = (x * _rms_scale(x) * g_ref[...]).astype(BF16)
    off = 0
    for ref, width in zip((uf_ref, cq_ref, ckv_ref, z_ref, xbc_ref, small_ref), _IN_WIDTHS):
        for c0 in range(0, width, 512):
            cw = min(512, width - c0)
            ref[:, c0:c0 + cw] = _dot(h, w_ref[:, off + c0:off + c0 + cw]).astype(ref.dtype)
        off += width


def _inproj(x, g, w, tm):
    t = x.shape[0]
    dtypes = (BF16, BF16, BF16, BF16, BF16, F32)
    return pl.pallas_call(
        _inproj_kernel,
        grid=(t // tm,),
        in_specs=[pl.BlockSpec((tm, D_MODEL), lambda i: (i, 0)),
                  _resident((1, D_MODEL)),
                  _resident((D_MODEL, PROJ_DIM))],
        out_specs=[pl.BlockSpec((tm, wd), lambda i: (i, 0)) for wd in _IN_WIDTHS],
        out_shape=[jax.ShapeDtypeStruct((t, wd), dt) for wd, dt in zip(_IN_WIDTHS, dtypes)],
        compiler_params=_params("parallel"),
    )(x, g, w)


def _dft_cos_sin(n):
    k = np.arange(n)
    ang = 2.0 * np.pi * ((k[:, None] * k[None, :]) % n) / n
    return np.cos(ang), np.sin(ang)


def _fnet_stage1_kernel(u_ref, m1_ref, twr_ref, twi_ref, yr_ref, yi_ref):
    n1 = u_ref.shape[0]
    y = _dot(m1_ref[...], u_ref[...])
    for s in range(FNET_S2_TILE):
        tr = twr_ref[:, s * LANES:(s + 1) * LANES]
        ti = twi_ref[:, s * LANES:(s + 1) * LANES]
        for q in range(FNET_DIM // LANES):
            c0 = s * FNET_DIM + q * LANES
            yr = y[:n1, c0:c0 + LANES]
            yi = y[n1:, c0:c0 + LANES]
            yr_ref[:, c0:c0 + LANES] = (yr * tr - yi * ti).astype(BF16)
            yi_ref[:, c0:c0 + LANES] = (yr * ti + yi * tr).astype(BF16)


def _fnet_stage2_kernel(yr_ref, yi_ref, m2_ref, mc_ref, o_ref):
    n2 = yr_ref.shape[1]
    for k1 in range(yr_ref.shape[0]):
        y = jnp.concatenate([yr_ref[k1], yi_ref[k1]], axis=0)
        h = _dot(m2_ref[...], y).astype(BF16)
        for g in range(FNET_GROUPS):
            c0 = g * FNET_GROUP_DIM
            hg = jnp.concatenate([h[:n2, c0:c0 + FNET_GROUP_DIM], h[n2:, c0:c0 + FNET_GROUP_DIM]],
                                 axis=1)
            o_ref[k1, :, c0:c0 + FNET_GROUP_DIM] = _dot(hg, mc_ref[...]).astype(BF16)


def _fourier(u):
    b, s, c = u.shape
    n2 = LANES
    n1 = s // n2
    assert n1 * n2 == s and n1 % 16 == 0 and n2 % FNET_S2_TILE == 0
    c1, s1 = _dft_cos_sin(n1)
    m1 = jnp.asarray(np.concatenate([c1, -s1], axis=0), BF16)
    c2, s2 = _dft_cos_sin(n2)
    m2 = jnp.asarray(np.block([[c2, s2], [-s2, c2]]), BF16)
    cc, sc = _dft_cos_sin(FNET_GROUP_DIM)
    mc = jnp.asarray(np.concatenate([cc, sc], axis=0), BF16)
    ang = 2.0 * np.pi * (np.arange(n1)[:, None] * np.arange(n2)[None, :]) / s
    scale = 1.0 / math.sqrt(s * FNET_GROUP_DIM)
    twr = jnp.repeat(jnp.asarray(np.cos(ang) * scale, F32), LANES, axis=1)
    twi = jnp.repeat(jnp.asarray(-np.sin(ang) * scale, F32), LANES, axis=1)

    cols = FNET_S2_TILE * c
    u2 = u.reshape(b, n1, n2 * c)
    yr, yi = pl.pallas_call(
        _fnet_stage1_kernel,
        grid=(b, n2 // FNET_S2_TILE),
        in_specs=[pl.BlockSpec((None, n1, cols), lambda i, j: (i, 0, j)),
                  _resident((2 * n1, n1)),
                  pl.BlockSpec((n1, FNET_S2_TILE * LANES), lambda i, j: (0, j)),
                  pl.BlockSpec((n1, FNET_S2_TILE * LANES), lambda i, j: (0, j))],
        out_specs=[pl.BlockSpec((None, n1, cols), lambda i, j: (i, 0, j))] * 2,
        out_shape=[jax.ShapeDtypeStruct((b, n1, n2 * c), BF16)] * 2,
        compiler_params=_params("parallel", "parallel"),
    )(u2, m1, twr, twi)

    yr = yr.reshape(b, n1, n2, c)
    yi = yi.reshape(b, n1, n2, c)
    out_t = pl.pallas_call(
        _fnet_stage2_kernel,
        grid=(b, n1 // FNET_K1_TILE),
        in_specs=[pl.BlockSpec((None, FNET_K1_TILE, n2, c), lambda i, j: (i, j, 0, 0)),
                  pl.BlockSpec((None, FNET_K1_TILE, n2, c), lambda i, j: (i, j, 0, 0)),
                  _resident((2 * n2, 2 * n2)),
                  _resident((2 * FNET_GROUP_DIM, FNET_GROUP_DIM))],
        out_specs=pl.BlockSpec((None, FNET_K1_TILE, n2, c), lambda i, j: (i, j, 0, 0)),
        out_shape=jax.ShapeDtypeStruct((b, n1, n2, c), BF16),
        compiler_params=_params("parallel", "parallel"),
    )(yr, yi, m2, mc)
    return out_t.transpose(0, 2, 1, 3).reshape(b, s, c)


def _mla_prep_kernel(cq_ref, ckv_ref, small_ref, cos_t_ref, sin_t_ref, cos_k_ref, sin_k_ref,
                     qg_ref, kvg_ref, wq_t_ref, wk_ref, wv_t_ref, q_t_ref, k_ref, v_t_ref):
    cq = cq_ref[...].astype(F32)
    cqn = (cq * _rms_scale(cq) * qg_ref[...]).astype(BF16)
    ckv = ckv_ref[...].astype(F32)
    ckvn = (ckv * _rms_scale(ckv) * kvg_ref[...]).astype(BF16)
    q_scale = math.log2(math.e) / math.sqrt(QK_HEAD)
    cos_t = cos_t_ref[...]
    sin_t = sin_t_ref[...]
    half = LANES // 2
    for h in range(MLA_HEADS):
        r0 = h * QK_PAD
        q_t = _dot_nt(wq_t_ref[r0:r0 + QK_PAD, :], cqn)
        q_t_ref[r0:r0 + QK_NOPE, :] = (q_t[:QK_NOPE] * q_scale).astype(BF16)
        blk = q_t[QK_NOPE:]
        swapped = jnp.concatenate([blk[half:], blk[:half]], axis=0)
        q_t_ref[r0 + QK_NOPE:r0 + QK_PAD, :] = ((blk * cos_t + swapped * sin_t) * q_scale).astype(BF16)

    small = small_ref[...]
    k_rope = (small * cos_k_ref[...] + pltpu.roll(small, half, axis=1) * sin_k_ref[...]).astype(BF16)
    for h in range(MLA_HEADS):
        c0 = h * QK_PAD
        k_ref[:, c0:c0 + QK_NOPE] = _dot(ckvn, wk_ref[:, h * QK_NOPE:(h + 1) * QK_NOPE]).astype(BF16)
        k_ref[:, c0 + QK_NOPE:c0 + QK_PAD] = k_rope
        v_t_ref[h, :V_HEAD, :] = _dot_nt(wv_t_ref[h * V_HEAD:(h + 1) * V_HEAD, :], ckvn).astype(BF16)
        v_t_ref[h, V_HEAD:, :] = jnp.ones((V_AUG - V_HEAD, v_t_ref.shape[2]), BF16)


def _rope_tables(s):
    pos = jnp.arange(s, dtype=F32)
    inv = ROPE_THETA ** (-jnp.arange(0, QK_ROPE, 2, dtype=F32) / QK_ROPE)
    ang = pos[:, None] * inv[None, :]
    cos, sin = jnp.cos(ang), jnp.sin(ang)
    zero = jnp.zeros_like(cos)
    cos_k = jnp.concatenate([cos, zero, cos, zero], axis=1)
    sin_k = jnp.concatenate([-sin, zero, sin, zero], axis=1)
    return cos_k, sin_k


def _mla_prep(cq, ckv, small, qg, kvg, wq_t, wk, wv_t, tm):
    b, s, _ = cq.shape
    cos_k, sin_k = _rope_tables(s)
    cos_t, sin_t = cos_k.T, sin_k.T
    tok = lambda i, j: (i, j, 0)
    tok_t = lambda i, j: (i, 0, j)
    return pl.pallas_call(
        _mla_prep_kernel,
        grid=(b, s // tm),
        in_specs=[pl.BlockSpec((None, tm, Q_LORA), tok),
                  pl.BlockSpec((None, tm, KV_LORA), tok),
                  pl.BlockSpec((None, tm, SMALL), tok),
                  pl.BlockSpec((LANES, tm), lambda i, j: (0, j)),
                  pl.BlockSpec((LANES, tm), lambda i, j: (0, j)),
                  pl.BlockSpec((tm, LANES), lambda i, j: (j, 0)),
                  pl.BlockSpec((tm, LANES), lambda i, j: (j, 0)),
                  _resident((1, Q_LORA)),
                  _resident((1, KV_LORA)),
                  _resident((MLA_HEADS * QK_PAD, Q_LORA)),
                  _resident((KV_LORA, MLA_HEADS * QK_NOPE)),
                  _resident((MLA_DIM, KV_LORA))],
        out_specs=[pl.BlockSpec((None, MLA_HEADS * QK_PAD, tm), tok_t),
                   pl.BlockSpec((None, tm, MLA_HEADS * QK_PAD), tok),
                   pl.BlockSpec((None, MLA_HEADS, None, V_AUG, tm), lambda i, j: (i, 0, j, 0, 0))],
        out_shape=[jax.ShapeDtypeStruct((b, MLA_HEADS * QK_PAD, s), BF16),
                   jax.ShapeDtypeStruct((b, s, MLA_HEADS * QK_PAD), BF16),
                   jax.ShapeDtypeStruct((b, MLA_HEADS, s // tm, V_AUG, tm), BF16)],
        compiler_params=_params("parallel", "parallel"),
    )(cq, ckv, small, cos_t, sin_t, cos_k, sin_k, qg, kvg, wq_t, wk, wv_t)


FLASH_RING = 4
FLASH_PASS = 4


def _flash_kernel(q_t_ref, k_ref, v_t_ref, o_ref, s0_ref, s1_ref, s2_ref, s3_ref, p0_ref, p1_ref,
                  mx_ref, m_ref, acc_ref):
    n, _, sub = v_t_ref.shape
    tq = q_t_ref.shape[1]
    s_refs = (s0_ref, s1_ref, s2_ref, s3_ref)
    p_refs = (p0_ref, p1_ref)
    m_ref[...] = jnp.full(m_ref.shape, -jnp.inf, F32)
    acc_ref[...] = jnp.zeros(acc_ref.shape, F32)

    def scores(c, slot):
        k0 = c * sub if isinstance(c, int) else pl.multiple_of(c * sub, sub)
        s = _dot(k_ref[pl.ds(k0, sub), :], q_t_ref[...])
        s_refs[slot][...] = s
        mx_ref[slot] = jnp.max(s.reshape(sub // F32_ROWS, F32_ROWS, tq), axis=0)

    def softmax(slot, p_ref):
        s_ref = s_refs[slot]
        m_prev = m_ref[...]
        m_new = jnp.maximum(m_prev, jnp.max(mx_ref[slot], axis=0, keepdims=True))
        alpha = jnp.exp2(m_prev - m_new)
        m_rows = jnp.broadcast_to(m_new, (BF16_ROWS, tq))
        for r in range(0, sub, BF16_ROWS):
            p_ref[r:r + BF16_ROWS, :] = jnp.exp2((s_ref[r:r + BF16_ROWS, :] - m_rows).astype(BF16))
        m_ref[...] = m_new
        return alpha

    blocks = min(FLASH_PASS, n)

    def ring_pass(c, lookahead, first):
        for j in range(blocks):
            slot = j % FLASH_RING
            if j < lookahead:
                scores(c + j + 2, (slot + 2) % FLASH_RING)
            if first and j == 0:
                softmax(0, p_refs[0])
                continue
            pv = _dot(v_t_ref[c + j - 1], p_refs[(j + 1) % 2][...])
            alpha = softmax(slot, p_refs[j % 2])
            acc_ref[...] = (acc_ref[...] + pv) * alpha

    scores(0, 0)
    scores(1, 1)
    last = n - blocks
    if last == 0:
        ring_pass(0, blocks - 2, True)
    else:
        ring_pass(0, blocks, True)

        def body(i, carry):
            ring_pass(i * blocks, blocks, False)
            return carry

        lax.fori_loop(1, last // blocks, body, 0)
        ring_pass(last, blocks - 2, False)
    acc = acc_ref[...] + _dot(v_t_ref[n - 1], p_refs[(n - 1) % 2][...])
    o_ref[...] = (acc[:V_HEAD] / acc[V_HEAD:V_HEAD + 1]).T.astype(o_ref.dtype)


def _flash(q_t, k, v_t, tq):
    b, s, _ = k.shape
    n, sub = v_t.shape[2], v_t.shape[4]
    assert n % FLASH_RING == 0 and n % min(FLASH_PASS, n) == 0 and n * sub == s
    return pl.pallas_call(
        _flash_kernel,
        grid=(b, MLA_HEADS, s // tq),
        in_specs=[pl.BlockSpec((None, QK_PAD, tq), lambda i, h, qi: (i, h, qi)),
                  pl.BlockSpec((None, s, QK_PAD), lambda i, h, qi: (i, 0, h)),
                  pl.BlockSpec((None, None, n, V_AUG, sub), lambda i, h, qi: (i, h, 0, 0, 0))],
        out_specs=pl.BlockSpec((None, tq, V_HEAD), lambda i, h, qi: (i, qi, h)),
        out_shape=jax.ShapeDtypeStruct((b, s, MLA_DIM), BF16),
        scratch_shapes=[pltpu.VMEM((sub, tq), F32)] * FLASH_RING
                       + [pltpu.VMEM((sub, tq), BF16)] * 2
                       + [pltpu.VMEM((FLASH_RING, F32_ROWS, tq), F32),
                          pltpu.VMEM((1, tq), F32), pltpu.VMEM((V_AUG, tq), F32)],
        compiler_params=_params("parallel", "parallel", "arbitrary"),
    )(q_t, k, v_t)


CONV_HALO = 16
CONV_PAD = 8


def _conv_kernel(cur_ref, prev_ref, next_ref, w_ref, b_ref, xs_ref, bm_ref, cm_ref, ext_ref):
    i = pl.program_id(1)
    tc = cur_ref.shape[0]
    prev = prev_ref[...].astype(F32)[CONV_HALO - CONV_PAD:]
    nxt = next_ref[...].astype(F32)[:CONV_PAD]
    ext_ref[0:CONV_PAD, :] = jnp.where(i > 0, prev, 0.0)
    ext_ref[CONV_PAD:CONV_PAD + tc, :] = cur_ref[...].astype(F32)
    ext_ref[CONV_PAD + tc:, :] = jnp.where(i < pl.num_programs(1) - 1, nxt, 0.0)
    acc = jnp.broadcast_to(b_ref[...], (tc, CONV_DIM))
    for k in range(CONV_K):
        acc = acc + w_ref[k:k + 1, :] * ext_ref[pl.ds(CONV_PAD - CONV_K // 2 + k, tc), :]
    y = acc / (1.0 + jnp.exp(-acc))
    xs_ref[...] = y[:, :SSD_DIM].astype(BF16)
    bm_ref[...] = y[:, SSD_DIM:SSD_DIM + BC_DIM].astype(BF16)
    cm_ref[...] = y[:, SSD_DIM + BC_DIM:].astype(BF16)


def _conv(xbc, w, bias, tc):
    b, s, _ = xbc.shape
    per = tc // CONV_HALO
    last = s // CONV_HALO - 1
    return pl.pallas_call(
        _conv_kernel,
        grid=(b, s // tc),
        in_specs=[pl.BlockSpec((None, tc, CONV_DIM), lambda i, j: (i, j, 0)),
                  pl.BlockSpec((None, CONV_HALO, CONV_DIM),
                               lambda i, j: (i, jnp.maximum(j * per - 1, 0), 0)),
                  pl.BlockSpec((None, CONV_HALO, CONV_DIM),
                               lambda i, j: (i, jnp.minimum((j + 1) * per, last), 0)),
                  _resident((8, CONV_DIM)),
                  _resident((1, CONV_DIM))],
        out_specs=[pl.BlockSpec((None, tc, SSD_DIM), lambda i, j: (i, j, 0)),
                   pl.BlockSpec((None, tc, BC_DIM), lambda i, j: (i, j, 0)),
                   pl.BlockSpec((None, tc, BC_DIM), lambda i, j: (i, j, 0))],
        out_shape=[jax.ShapeDtypeStruct((b, s, SSD_DIM), BF16),
                   jax.ShapeDtypeStruct((b, s, BC_DIM), BF16),
                   jax.ShapeDtypeStruct((b, s, BC_DIM), BF16)],
        scratch_shapes=[pltpu.VMEM((tc + 2 * CONV_PAD, CONV_DIM), F32)],
        compiler_params=_params("parallel", "parallel"),
    )(xbc, xbc, xbc, w, bias)


class _SsdChain:
    def __init__(self, direction, r0, xs_ref, bm_ref, cm_ref, small_ref, y_ref):
        self.d, self.r0 = direction, r0
        self.rows = slice(r0, r0 + CHUNK)
        self.xs_ref, self.bm_ref, self.cm_ref, self.small_ref, self.y_ref = (
            xs_ref, bm_ref, cm_ref, small_ref, y_ref)

    def step_size(self, sel_ref, bias_ref):
        raw = _dot_exact_rhs01(self.small_ref[self.rows, :], sel_ref[self.d]) + bias_ref[self.d]
        self.dt = jnp.maximum(raw, 0.0) + jnp.log1p(jnp.exp(-jnp.abs(raw)))

    def decay_log(self, tri_ref, a_heads):
        self.g = _dot_split2_lhs01(tri_ref[self.d], self.dt * a_heads[self.d])

    def widen(self, e64_ref):
        self.g_t = self.g.T
        self.g_wide = _dot_split2_rhs01(self.g, e64_ref[...])
        self.dt_wide = _dot_split2_rhs01(self.dt, e64_ref[...])

    def within_chunk(self):
        reverse = self.d == 1
        row = lax.broadcasted_iota(jnp.int32, (CHUNK, CHUNK), 0)
        col = lax.broadcasted_iota(jnp.int32, (CHUNK, CHUNK), 1)
        mask = (row <= col) if reverse else (row >= col)
        end = 0 if reverse else CHUNK - 1
        g, g_t, g_wide = self.g, self.g_t, self.g_wide
        self.g_end = g_wide[end:end + 1, :]
        xd = self.xs_ref[self.rows, :].astype(F32) * self.dt_wide
        first = lax.broadcasted_iota(jnp.int32, (CHUNK, SSD_DIM), 1) % LANES < SSD_HEAD_DIM
        xd_first = jnp.where(first, xd, 0.0).astype(BF16)
        xd_second = jnp.where(first, 0.0, xd).astype(BF16)
        self.xdw = (xd * jnp.exp(self.g_end - g_wide)).astype(BF16)
        self.scale_off = jnp.exp(g_wide)
        bm = self.bm_ref[self.rows, :]
        cm = self.cm_ref[self.rows, :]
        self.b_g, self.c_g, self.y_diag = [], [], []
        for grp in range(SSD_GROUPS):
            n0 = grp * SSD_STATE
            b_g = bm[:, n0:n0 + SSD_STATE]
            c_g = cm[:, n0:n0 + SSD_STATE]
            self.b_g.append(b_g)
            self.c_g.append(c_g)
            cb = _dot_nt(c_g, b_g)
            for pr in range(SSD_HPG // 2):
                h0 = grp * SSD_HPG + 2 * pr
                ws = []
                for h in (h0, h0 + 1):
                    diff = g[:, h:h + 1] - g_t[h:h + 1, :]
                    ws.append((cb * jnp.exp(jnp.where(mask, diff, -jnp.inf))).astype(BF16))
                p0 = h0 * SSD_HEAD_DIM
                rhs = jnp.concatenate([xd_first[:, p0:p0 + LANES], xd_second[:, p0:p0 + LANES]],
                                      axis=0)
                self.y_diag.append(_dot(jnp.concatenate(ws, axis=1), rhs))

    def across_chunks(self, state_ref):
        for grp in range(SSD_GROUPS):
            d0 = grp * SSD_GROUP_DIM
            state = state_ref[self.d, grp]
            y_off = (_dot(self.c_g[grp], state.astype(BF16))
                     * self.scale_off[:, d0:d0 + SSD_GROUP_DIM])
            for pr in range(SSD_HPG // 2):
                p0 = d0 + pr * LANES
                y = self.y_diag[grp * (SSD_HPG // 2) + pr] + y_off[:, pr * LANES:(pr + 1) * LANES]
                self.y_ref[self.rows, p0:p0 + LANES] = y.astype(self.y_ref.dtype)
            state_ref[self.d, grp] = (state * jnp.exp(self.g_end[:, d0:d0 + SSD_GROUP_DIM])
                                      + _dot_tn(self.b_g[grp], self.xdw[:, d0:d0 + SSD_GROUP_DIM]))


def _ssd_kernel(xs_f_ref, bm_f_ref, cm_f_ref, small_f_ref, xs_b_ref, bm_b_ref, cm_b_ref, small_b_ref,
                sel_ref, e64_ref, tri_ref, bias_ref, alog_ref, yf_ref, yb_ref, state_ref, *, chunks):
    @pl.when(pl.program_id(1) == 0)
    def _():
        state_ref[...] = jnp.zeros(state_ref.shape, F32)

    a_heads = -jnp.exp(alog_ref[...])
    chains = []
    for c in range(chunks):
        chains.append(_SsdChain(0, c * CHUNK, xs_f_ref, bm_f_ref, cm_f_ref, small_f_ref, yf_ref))
        chains.append(_SsdChain(1, (chunks - 1 - c) * CHUNK, xs_b_ref, bm_b_ref, cm_b_ref,
                                small_b_ref, yb_ref))
    for ch in chains:
        ch.step_size(sel_ref, bias_ref)
    for ch in chains:
        ch.decay_log(tri_ref, a_heads)
    for ch in chains:
        ch.widen(e64_ref)
    for ch in chains:
        ch.within_chunk()
    for ch in chains:
        ch.across_chunks(state_ref)


def _ssd_consts():
    sel = np.zeros((2, LANES, LANES), np.float32)
    for d, base in enumerate((SMALL_DTF, SMALL_DTB)):
        sel[d, base + np.arange(SSD_HEADS), np.arange(SSD_HEADS)] = 1.0
    e64 = np.zeros((LANES, SSD_DIM), np.float32)
    for h in range(SSD_HEADS):
        e64[h, h * SSD_HEAD_DIM:(h + 1) * SSD_HEAD_DIM] = 1.0
    lower = np.tril(np.ones((CHUNK, CHUNK), np.float32))
    tri = np.stack([lower, lower.T])
    return jnp.asarray(sel, BF16), jnp.asarray(e64, BF16), jnp.asarray(tri, BF16)


def _ssd(xs, bm, cm, small, dt_bias_f, dt_bias_b, a_log_f, a_log_b, chunks):
    b, s, _ = xs.shape
    rows = chunks * CHUNK
    n = s // rows
    sel, e64, tri = _ssd_consts()
    pad = lambda f, bk: jnp.pad(jnp.stack([f, bk]).astype(F32),
                                ((0, 0), (0, LANES - SSD_HEADS))).reshape(2, 1, LANES)
    fwd = lambda i, j: (i, j, 0)
    bwd = lambda i, j: (i, n - 1 - j, 0)
    tiles = lambda tok: [pl.BlockSpec((None, rows, SSD_DIM), tok),
                         pl.BlockSpec((None, rows, BC_DIM), tok),
                         pl.BlockSpec((None, rows, BC_DIM), tok),
                         pl.BlockSpec((None, rows, SMALL), tok)]
    return pl.pallas_call(
        functools.partial(_ssd_kernel, chunks=chunks),
        grid=(b, n),
        in_specs=tiles(fwd) + tiles(bwd) + [
            _resident((2, LANES, LANES)),
            _resident((LANES, SSD_DIM)),
            _resident((2, CHUNK, CHUNK)),
            _resident((2, 1, LANES)),
            _resident((2, 1, LANES))],
        out_specs=[pl.BlockSpec((None, rows, SSD_DIM), fwd), pl.BlockSpec((None, rows, SSD_DIM), bwd)],
        out_shape=[jax.ShapeDtypeStruct((b, s, SSD_DIM), BF16)] * 2,
        scratch_shapes=[pltpu.VMEM((2, SSD_GROUPS, SSD_STATE, SSD_GROUP_DIM), F32)],
        compiler_params=_params("parallel", "arbitrary"),
    )(xs, bm, cm, small, xs, bm, cm, small, sel, e64, tri,
      pad(dt_bias_f, dt_bias_b), pad(a_log_f, a_log_b))


def _outproj_kernel(x_ref, fo_ref, mo_ref, yf_ref, yb_ref, xs_ref, z_ref, dskip_ref, ng_ref,
                    w_ref, pg_ref, o_ref):
    y = (yf_ref[...].astype(F32) + yb_ref[...].astype(F32)
         + xs_ref[...].astype(F32) * dskip_ref[...])
    z = z_ref[...].astype(F32)
    y = y * (z / (1.0 + jnp.exp(-z)))
    parts = []
    for grp in range(SSD_GROUPS):
        yg = y[:, grp * SSD_GROUP_DIM:(grp + 1) * SSD_GROUP_DIM]
        parts.append(yg * _rms_scale(yg))
    y = (jnp.concatenate(parts, axis=1) * ng_ref[...]).astype(BF16)
    mix = (_dot(fo_ref[...], w_ref[0:FNET_DIM, :])
           + _dot(mo_ref[...], w_ref[FNET_DIM:FNET_DIM + MLA_DIM, :])
           + _dot(y, w_ref[FNET_DIM + MLA_DIM:, :]))
    o_ref[...] = x_ref[...] + mix * _rms_scale(mix) * pg_ref[...]


def _outproj(x, fo, mo, yf, yb, xs, z, dskip, ng, w, pg, tm):
    t = x.shape[0]
    rows = lambda wd: pl.BlockSpec((tm, wd), lambda i: (i, 0))
    return pl.pallas_call(
        _outproj_kernel,
        grid=(t // tm,),
        in_specs=[rows(D_MODEL), rows(FNET_DIM), rows(MLA_DIM), rows(SSD_DIM), rows(SSD_DIM),
                  rows(SSD_DIM), rows(SSD_DIM), _resident((1, SSD_DIM)), _resident((1, SSD_DIM)),
                  _resident((D_MODEL, D_MODEL)), _resident((1, D_MODEL))],
        out_specs=rows(D_MODEL),
        out_shape=jax.ShapeDtypeStruct((t, D_MODEL), F32),
        compiler_params=_params("parallel"),
    )(x, fo, mo, yf, yb, xs, z, dskip, ng, w, pg)


FFN_SUB = 1024


def _ffn_kernel(x_ref, g_ref, w1_ref, w2_ref, pg_ref, o_ref, h_ref):
    j = pl.program_id(1)

    @pl.when(j == 0)
    def _():
        x = x_ref[...]
        h_ref[...] = (x * _rms_scale(x) * g_ref[...]).astype(BF16)
        o_ref[...] = jnp.zeros(o_ref.shape, F32)

    for c0 in range(0, w1_ref.shape[1], FFN_SUB):
        a = jnp.maximum(_dot(h_ref[...], w1_ref[:, c0:c0 + FFN_SUB]), 0.0)
        o_ref[...] += _dot((a * a).astype(BF16), w2_ref[c0:c0 + FFN_SUB, :])

    @pl.when(j == pl.num_programs(1) - 1)
    def _():
        f = o_ref[...]
        o_ref[...] = x_ref[...] + f * _rms_scale(f) * pg_ref[...]


def _ffn(x, g, w1, w2, pg, layer, tm, tf):
    t = x.shape[0]
    return pl.pallas_call(
        _ffn_kernel,
        grid=(t // tm, D_FF // tf),
        in_specs=[pl.BlockSpec((tm, D_MODEL), lambda i, j: (i, 0)),
                  _resident((1, D_MODEL)),
                  pl.BlockSpec((None, D_MODEL, tf), lambda i, j: (layer, 0, j)),
                  pl.BlockSpec((None, tf, D_MODEL), lambda i, j: (layer, j, 0)),
                  _resident((1, D_MODEL))],
        out_specs=pl.BlockSpec((tm, D_MODEL), lambda i, j: (i, 0)),
        out_shape=jax.ShapeDtypeStruct((t, D_MODEL), F32),
        scratch_shapes=[pltpu.VMEM((tm, D_MODEL), BF16)],
        compiler_params=_params("parallel", "arbitrary"),
    )(x, g, w1, w2, pg)


def _prep_layer(pre_mix_g, w_in, q_norm_g, w_q_up, kv_norm_g, w_kv_up, conv_w, conv_b,
                dt_bias_f, dt_bias_b, a_log_f, a_log_b, d_skip, ssd_norm_g, w_out,
                post_mix_g, pre_ffn_g, w_ff1, w_ff2, post_ffn_g):
    o_cq = FNET_DIM
    o_ckv = o_cq + Q_LORA
    o_pe = o_ckv + KV_LORA
    o_z = o_pe + QK_ROPE
    o_xbc = o_z + SSD_DIM
    o_dt = o_xbc + CONV_DIM
    zeros = lambda n: jnp.zeros((D_MODEL, n), w_in.dtype)
    small_cols = jnp.concatenate([
        w_in[:, o_pe:o_pe + ROPE_HALF],
        w_in[:, o_dt:o_dt + 2 * SSD_HEADS],
        zeros(SMALL_PE2 - ROPE_HALF - 2 * SSD_HEADS),
        w_in[:, o_pe + ROPE_HALF:o_pe + QK_ROPE],
        zeros(LANES - SMALL_PE2 - ROPE_HALF)], axis=1)
    w_in_r = jnp.concatenate([w_in[:, :o_pe], w_in[:, o_z:o_dt], small_cols], axis=1).astype(BF16)

    wq = w_q_up.reshape(Q_LORA, MLA_HEADS, QK_HEAD)
    zq = jnp.zeros((Q_LORA, MLA_HEADS, ROPE_HALF), w_q_up.dtype)
    wq = jnp.concatenate([wq[..., :QK_NOPE + ROPE_HALF], zq, wq[..., QK_NOPE + ROPE_HALF:], zq], axis=-1)
    wq_t = wq.reshape(Q_LORA, MLA_HEADS * QK_PAD).T.astype(BF16)
    wkv = w_kv_up.reshape(KV_LORA, MLA_HEADS, QK_NOPE + V_HEAD)
    wk = wkv[..., :QK_NOPE].reshape(KV_LORA, MLA_HEADS * QK_NOPE).astype(BF16)
    wv_t = wkv[..., QK_NOPE:].reshape(KV_LORA, MLA_DIM).T.astype(BF16)

    row = lambda v: v.astype(F32).reshape(1, -1)
    return dict(
        pre_mix_g=row(pre_mix_g), w_in=w_in_r, q_norm_g=row(q_norm_g), kv_norm_g=row(kv_norm_g),
        wq_t=wq_t, wk=wk, wv_t=wv_t,
        conv_w=jnp.pad(conv_w.astype(F32), ((0, 8 - CONV_K), (0, 0))), conv_b=row(conv_b),
        dt_bias_f=dt_bias_f, dt_bias_b=dt_bias_b, a_log_f=a_log_f, a_log_b=a_log_b,
        d_skip=row(jnp.repeat(d_skip, SSD_HEAD_DIM)), ssd_norm_g=row(ssd_norm_g),
        w_out=w_out.astype(BF16), post_mix_g=row(post_mix_g), pre_ffn_g=row(pre_ffn_g),
        post_ffn_g=row(post_ffn_g))


def _tile(n, want):
    return want if n % want == 0 else n


def _layer(x, p):
    b, s, _ = x.shape
    t = b * s
    xt = x.reshape(t, D_MODEL)
    uf, cq, ckv, z, xbc, small = _inproj(xt, p["pre_mix_g"], p["w_in"], _tile(t, 512))
    small3 = small.reshape(b, s, SMALL)

    fo = _fourier(uf.reshape(b, s, FNET_DIM)).reshape(t, FNET_DIM)

    q_t, k, v_t = _mla_prep(cq.reshape(b, s, Q_LORA), ckv.reshape(b, s, KV_LORA), small3,
                            p["q_norm_g"], p["kv_norm_g"], p["wq_t"], p["wk"], p["wv_t"],
                            _tile(s, 1024) if s >= 4096 else _tile(s, 512))
    mo = _flash(q_t, k, v_t, _tile(s, 512) if s >= 4096 else _tile(s, 1024)).reshape(t, MLA_DIM)

    xs, bm, cm = _conv(xbc.reshape(b, s, CONV_DIM), p["conv_w"], p["conv_b"], _tile(s, 512))
    chunks = 8 if s % (8 * CHUNK) == 0 else 1
    yf, yb = _ssd(xs, bm, cm, small3, p["dt_bias_f"], p["dt_bias_b"], p["a_log_f"], p["a_log_b"],
                  chunks)

    x1 = _outproj(xt, fo, mo, yf.reshape(t, SSD_DIM), yb.reshape(t, SSD_DIM),
                  xs.reshape(t, SSD_DIM), z, p["d_skip"], p["ssd_norm_g"], p["w_out"],
                  p["post_mix_g"], _tile(t, 512))
    x2 = _ffn(x1, p["pre_ffn_g"], p["w_ff1"], p["w_ff2"], p["post_ffn_g"], p["layer"],
              _tile(t, 512), 2048)
    return x2.reshape(b, s, D_MODEL)


def kernel(x_prompt, x_sample, pre_mix_g, w_in, q_norm_g, w_q_up, kv_norm_g, w_kv_up, conv_w, conv_b, dt_bias_f, dt_bias_b, a_log_f, a_log_b, d_skip, ssd_norm_g, w_out, post_mix_g, pre_ffn_g, w_ff1, w_ff2, post_ffn_g):
    weights = (pre_mix_g, w_in, q_norm_g, w_q_up, kv_norm_g, w_kv_up, conv_w, conv_b,
               dt_bias_f, dt_bias_b, a_log_f, a_log_b, d_skip, ssd_norm_g, w_out,
               post_mix_g, pre_ffn_g, w_ff1, w_ff2, post_ffn_g)
    depth = w_in.shape[0]
    layers = [_prep_layer(*[w[l] for w in weights]) for l in range(depth)]
    w_ff1_b, w_ff2_b = w_ff1.astype(BF16), w_ff2.astype(BF16)
    for l, p in enumerate(layers):
        p.update(layer=l, w_ff1=w_ff1_b, w_ff2=w_ff2_b)

    def run_trunk(x):
        for p in layers:
            x = _layer(x, p)
        return x

    return (run_trunk(x_prompt), run_trunk(x_sample))
```

```python
import functools
import math

import jax
import jax.numpy as jnp
import numpy as np
from jax import lax
from jax.experimental import pallas as pl
from jax.experimental.pallas import tpu as pltpu

F32 = jnp.float32
BF16 = jnp.bfloat16

D_MODEL = 2048
FNET_GROUPS = 4
FNET_GROUP_DIM = 128
FNET_DIM = FNET_GROUPS * FNET_GROUP_DIM
MLA_HEADS = 6
Q_LORA = 512
KV_LORA = 512
QK_NOPE = 128
QK_ROPE = 64
ROPE_HALF = QK_ROPE // 2
V_HEAD = 128
V_AUG = V_HEAD + 16
QK_HEAD = QK_NOPE + QK_ROPE
QK_PAD = 256
MLA_DIM = MLA_HEADS * V_HEAD
ROPE_THETA = 10000.0
SSD_HEADS = 12
SSD_HEAD_DIM = 64
SSD_DIM = SSD_HEADS * SSD_HEAD_DIM
SSD_GROUPS = 2
SSD_HPG = SSD_HEADS // SSD_GROUPS
SSD_GROUP_DIM = SSD_DIM // SSD_GROUPS
SSD_STATE = 128
CONV_K = 5
CHUNK = 128
CONV_DIM = SSD_DIM + 2 * SSD_GROUPS * SSD_STATE
BC_DIM = SSD_GROUPS * SSD_STATE
D_FF = 4 * D_MODEL
EPS = 1e-6

LANES = 128
F32_ROWS = 8
BF16_ROWS = 16
SMALL = LANES
SMALL_DTF = ROPE_HALF
SMALL_DTB = ROPE_HALF + SSD_HEADS
SMALL_PE2 = LANES // 2
PROJ_DIM = FNET_DIM + Q_LORA + KV_LORA + SSD_DIM + CONV_DIM + SMALL

VMEM_LIMIT = 60 * 1024 * 1024
FNET_S2_TILE = 8
FNET_K1_TILE = 8


def _params(*sem):
    return pltpu.CompilerParams(dimension_semantics=sem, vmem_limit_bytes=VMEM_LIMIT)


def _dot(a, b):
    return jnp.dot(a, b, preferred_element_type=F32)


def _dot_nt(a, b):
    return lax.dot_general(a, b, (((1,), (1,)), ((), ())), preferred_element_type=F32)


def _dot_tn(a, b):
    return lax.dot_general(a, b, (((0,), (0,)), ((), ())), preferred_element_type=F32)


def _rms_scale(x):
    return lax.rsqrt(jnp.mean(x * x, axis=-1, keepdims=True) + EPS)


def _split3(x):
    hi = x.astype(BF16)
    r = x - hi.astype(F32)
    mid = r.astype(BF16)
    lo = (r - mid.astype(F32)).astype(BF16)
    return hi, mid, lo


def _dot_exact_rhs01(x, m01):
    hi, mid, lo = _split3(x)
    return _dot(hi, m01) + _dot(mid, m01) + _dot(lo, m01)


def _split2(x):
    hi = x.astype(BF16)
    return hi, (x - hi.astype(F32)).astype(BF16)


def _dot_split2_rhs01(x, m01):
    hi, lo = _split2(x)
    return _dot(hi, m01) + _dot(lo, m01)


def _dot_split2_lhs01(m01, x):
    hi, lo = _split2(x)
    return _dot(m01, hi) + _dot(m01, lo)


def _resident(shape):
    nd = len(shape)
    return pl.BlockSpec(shape, lambda *_: (0,) * nd, pipeline_mode=pl.Buffered(1))


_IN_WIDTHS = (FNET_DIM, Q_LORA, KV_LORA, SSD_DIM, CONV_DIM, SMALL)


def _inproj_kernel(x_ref, g_ref, w_ref, uf_ref, cq_ref, ckv_ref, z_ref, xbc_ref, small_ref):
    x = x_ref[...]
    h = (x * _rms_scale(x) * g_ref[...]).astype(BF16)
    off = 0
    for ref, width in zip((uf_ref, cq_ref, ckv_ref, z_ref, xbc_ref, small_ref), _IN_WIDTHS):
        for c0 in range(0, width, 512):
            cw = min(512, width - c0)
            ref[:, c0:c0 + cw] = _dot(h, w_ref[:, off + c0:off + c0 + cw]).astype(ref.dtype)
        off += width


def _inproj(x, g, w, tm):
    t = x.shape[0]
    dtypes = (BF16, BF16, BF16, BF16, BF16, F32)
    return pl.pallas_call(
        _inproj_kernel,
        grid=(t // tm,),
        in_specs=[pl.BlockSpec((tm, D_MODEL), lambda i: (i, 0)),
                  _resident((1, D_MODEL)),
                  _resident((D_MODEL, PROJ_DIM))],
        out_specs=[pl.BlockSpec((tm, wd), lambda i: (i, 0)) for wd in _IN_WIDTHS],
        out_shape=[jax.ShapeDtypeStruct((t, wd), dt) for wd, dt in zip(_IN_WIDTHS, dtypes)],
        compiler_params=_params("parallel"),
    )(x, g, w)


def _dft_cos_sin(n):
    k = np.arange(n)
    ang = 2.0 * np.pi * ((k[:, None] * k[None, :]) % n) / n
    return np.cos(ang), np.sin(ang)


def _fnet_stage1_kernel(u_ref, m1_ref, twr_ref, twi_ref, yr_ref, yi_ref):
    n1 = u_ref.shape[0]
    y = _dot(m1_ref[...], u_ref[...])
    for s in range(FNET_S2_TILE):
        tr = twr_ref[:, s * LANES:(s + 1) * LANES]
        ti = twi_ref[:, s * LANES:(s + 1) * LANES]
        for q in range(FNET_DIM // LANES):
            c0 = s * FNET_DIM + q * LANES
            yr = y[:n1, c0:c0 + LANES]
            yi = y[n1:, c0:c0 + LANES]
            yr_ref[:, c0:c0 + LANES] = (yr * tr - yi * ti).astype(BF16)
            yi_ref[:, c0:c0 + LANES] = (yr * ti + yi * tr).astype(BF16)


def _fnet_stage2_kernel(yr_ref, yi_ref, m2_ref, mc_ref, o_ref):
    n2 = yr_ref.shape[1]
    for k1 in range(yr_ref.shape[0]):
        y = jnp.concatenate([yr_ref[k1], yi_ref[k1]], axis=0)
        h = _dot(m2_ref[...], y).astype(BF16)
        for g in range(FNET_GROUPS):
            c0 = g * FNET_GROUP_DIM
            hg = jnp.concatenate([h[:n2, c0:c0 + FNET_GROUP_DIM], h[n2:, c0:c0 + FNET_GROUP_DIM]],
                                 axis=1)
            o_ref[k1, :, c0:c0 + FNET_GROUP_DIM] = _dot(hg, mc_ref[...]).astype(BF16)


def _fourier(u):
    b, s, c = u.shape
    n2 = LANES
    n1 = s // n2
    assert n1 * n2 == s and n1 % 16 == 0 and n2 % FNET_S2_TILE == 0
    c1, s1 = _dft_cos_sin(n1)
    m1 = jnp.asarray(np.concatenate([c1, -s1], axis=0), BF16)
    c2, s2 = _dft_cos_sin(n2)
    m2 = jnp.asarray(np.block([[c2, s2], [-s2, c2]]), BF16)
    cc, sc = _dft_cos_sin(FNET_GROUP_DIM)
    mc = jnp.asarray(np.concatenate([cc, sc], axis=0), BF16)
    ang = 2.0 * np.pi * (np.arange(n1)[:, None] * np.arange(n2)[None, :]) / s
    scale = 1.0 / math.sqrt(s * FNET_GROUP_DIM)
    twr = jnp.repeat(jnp.asarray(np.cos(ang) * scale, F32), LANES, axis=1)
    twi = jnp.repeat(jnp.asarray(-np.sin(ang) * scale, F32), LANES, axis=1)

    cols = FNET_S2_TILE * c
    u2 = u.reshape(b, n1, n2 * c)
    yr, yi = pl.pallas_call(
        _fnet_stage1_kernel,
        grid=(b, n2 // FNET_S2_TILE),
        in_specs=[pl.BlockSpec((None, n1, cols), lambda i, j: (i, 0, j)),
                  _resident((2 * n1, n1)),
                  pl.BlockSpec((n1, FNET_S2_TILE * LANES), lambda i, j: (0, j)),
                  pl.BlockSpec((n1, FNET_S2_TILE * LANES), lambda i, j: (0, j))],
        out_specs=[pl.BlockSpec((None, n1, cols), lambda i, j: (i, 0, j))] * 2,
        out_shape=[jax.ShapeDtypeStruct((b, n1, n2 * c), BF16)] * 2,
        compiler_params=_params("parallel", "parallel"),
    )(u2, m1, twr, twi)

    yr = yr.reshape(b, n1, n2, c)
    yi = yi.reshape(b, n1, n2, c)
    out_t = pl.pallas_call(
        _fnet_stage2_kernel,
        grid=(b, n1 // FNET_K1_TILE),
        in_specs=[pl.BlockSpec((None, FNET_K1_TILE, n2, c), lambda i, j: (i, j, 0, 0)),
                  pl.BlockSpec((None, FNET_K1_TILE, n2, c), lambda i, j: (i, j, 0, 0)),
                  _resident((2 * n2, 2 * n2)),
                  _resident((2 * FNET_GROUP_DIM, FNET_GROUP_DIM))],
        out_specs=pl.BlockSpec((None, FNET_K1_TILE, n2, c), lambda i, j: (i, j, 0, 0)),
        out_shape=jax.ShapeDtypeStruct((b, n1, n2, c), BF16),
        compiler_params=_params("parallel", "parallel"),
    )(yr, yi, m2, mc)
    return out_t.transpose(0, 2, 1, 3).reshape(b, s, c)


def _mla_prep_kernel(cq_ref, ckv_ref, small_ref, cos_t_ref, sin_t_ref, cos_k_ref, sin_k_ref,
                     qg_ref, kvg_ref, wq_t_ref, wk_ref, wv_t_ref, q_t_ref, k_ref, v_t_ref):
    cq = cq_ref[...].astype(F32)
    cqn = (cq * _rms_scale(cq) * qg_ref[...]).astype(BF16)
    ckv = ckv_ref[...].astype(F32)
    ckvn = (ckv * _rms_scale(ckv) * kvg_ref[...]).astype(BF16)
    q_scale = math.log2(math.e) / math.sqrt(QK_HEAD)
    cos_t = cos_t_ref[...]
    sin_t = sin_t_ref[...]
    half = LANES // 2
    for h in range(MLA_HEADS):
        r0 = h * QK_PAD
        q_t = _dot_nt(wq_t_ref[r0:r0 + QK_PAD, :], cqn)
        q_t_ref[r0:r0 + QK_NOPE, :] = (q_t[:QK_NOPE] * q_scale).astype(BF16)
        blk = q_t[QK_NOPE:]
        swapped = jnp.concatenate([blk[half:], blk[:half]], axis=0)
        q_t_ref[r0 + QK_NOPE:r0 + QK_PAD, :] = ((blk * cos_t + swapped * sin_t) * q_scale).astype(BF16)

    small = small_ref[...]
    k_rope = (small * cos_k_ref[...] + pltpu.roll(small, half, axis=1) * sin_k_ref[...]).astype(BF16)
    for h0 in range(0, MLA_HEADS, 2):
        k_pair = _dot(ckvn, wk_ref[:, h0 * QK_NOPE:(h0 + 2) * QK_NOPE]).astype(BF16)
        for h in (h0, h0 + 1):
            c0 = h * QK_PAD
            k_ref[:, c0:c0 + QK_NOPE] = k_pair[:, (h - h0) * QK_NOPE:(h - h0 + 1) * QK_NOPE]
            k_ref[:, c0 + QK_NOPE:c0 + QK_PAD] = k_rope
    for h in range(MLA_HEADS):
        v_t_ref[h, :V_HEAD, :] = _dot_nt(wv_t_ref[h * V_HEAD:(h + 1) * V_HEAD, :], ckvn).astype(BF16)
        v_t_ref[h, V_HEAD:, :] = jnp.ones((V_AUG - V_HEAD, v_t_ref.shape[2]), BF16)


def _rope_tables(s):
    pos = jnp.arange(s, dtype=F32)
    inv = ROPE_THETA ** (-jnp.arange(0, QK_ROPE, 2, dtype=F32) / QK_ROPE)
    ang = pos[:, None] * inv[None, :]
    cos, sin = jnp.cos(ang), jnp.sin(ang)
    zero = jnp.zeros_like(cos)
    cos_k = jnp.concatenate([cos, zero, cos, zero], axis=1)
    sin_k = jnp.concatenate([-sin, zero, sin, zero], axis=1)
    return cos_k, sin_k


def _mla_prep(cq, ckv, small, qg, kvg, wq_t, wk, wv_t, tm):
    b, s, _ = cq.shape
    cos_k, sin_k = _rope_tables(s)
    cos_t, sin_t = cos_k.T, sin_k.T
    tok = lambda i, j: (i, j, 0)
    tok_t = lambda i, j: (i, 0, j)
    return pl.pallas_call(
        _mla_prep_kernel,
        grid=(b, s // tm),
        in_specs=[pl.BlockSpec((None, tm, Q_LORA), tok),
                  pl.BlockSpec((None, tm, KV_LORA), tok),
                  pl.BlockSpec((None, tm, SMALL), tok),
                  pl.BlockSpec((LANES, tm), lambda i, j: (0, j)),
                  pl.BlockSpec((LANES, tm), lambda i, j: (0, j)),
                  pl.BlockSpec((tm, LANES), lambda i, j: (j, 0)),
                  pl.BlockSpec((tm, LANES), lambda i, j: (j, 0)),
                  _resident((1, Q_LORA)),
                  _resident((1, KV_LORA)),
                  _resident((MLA_HEADS * QK_PAD, Q_LORA)),
                  _resident((KV_LORA, MLA_HEADS * QK_NOPE)),
                  _resident((MLA_DIM, KV_LORA))],
        out_specs=[pl.BlockSpec((None, MLA_HEADS * QK_PAD, tm), tok_t),
                   pl.BlockSpec((None, tm, MLA_HEADS * QK_PAD), tok),
                   pl.BlockSpec((None, MLA_HEADS, None, V_AUG, tm), lambda i, j: (i, 0, j, 0, 0))],
        out_shape=[jax.ShapeDtypeStruct((b, MLA_HEADS * QK_PAD, s), BF16),
                   jax.ShapeDtypeStruct((b, s, MLA_HEADS * QK_PAD), BF16),
                   jax.ShapeDtypeStruct((b, MLA_HEADS, s // tm, V_AUG, tm), BF16)],
        compiler_params=_params("parallel", "parallel"),
    )(cq, ckv, small, cos_t, sin_t, cos_k, sin_k, qg, kvg, wq_t, wk, wv_t)


FLASH_RING = 4
FLASH_AHEAD = 2
FLASH_PASS = 4


def _flash_kernel(q_t_ref, k_ref, v_t_ref, o_ref, s0_ref, s1_ref, s2_ref, s3_ref, p0_ref, p1_ref,
                  mx_ref, m_ref, acc_ref):
    n, _, sub = v_t_ref.shape
    tq = q_t_ref.shape[1]
    s_refs = (s0_ref, s1_ref, s2_ref, s3_ref)
    p_refs = (p0_ref, p1_ref)
    m_ref[...] = jnp.full(m_ref.shape, -jnp.inf, F32)
    acc_ref[...] = jnp.zeros(acc_ref.shape, F32)

    def scores(c, slot):
        k0 = c * sub if isinstance(c, int) else pl.multiple_of(c * sub, sub)
        s = _dot(k_ref[pl.ds(k0, sub), :], q_t_ref[...])
        s_refs[slot][...] = s
        mx_ref[slot] = jnp.max(s.reshape(sub // F32_ROWS, F32_ROWS, tq), axis=0)

    def softmax(slot, p_ref):
        s_ref = s_refs[slot]
        m_prev = m_ref[...]
        m_new = jnp.maximum(m_prev, jnp.max(mx_ref[slot], axis=0, keepdims=True))
        alpha = jnp.exp2(m_prev - m_new)
        m_rows = jnp.broadcast_to(m_new, (BF16_ROWS, tq))
        for r in range(0, sub, BF16_ROWS):
            p_ref[r:r + BF16_ROWS, :] = jnp.exp2((s_ref[r:r + BF16_ROWS, :] - m_rows).astype(BF16))
        m_ref[...] = m_new
        return alpha

    blocks = min(FLASH_PASS, n)

    def ring_pass(c, lookahead, first):
        for j in range(blocks):
            slot = j % FLASH_RING
            if j < lookahead:
                scores(c + j + FLASH_AHEAD, (slot + FLASH_AHEAD) % FLASH_RING)
            if first and j == 0:
                softmax(0, p_refs[0])
                continue
            pv = _dot(v_t_ref[c + j - 1], p_refs[(j + 1) % 2][...])
            alpha = softmax(slot, p_refs[j % 2])
            acc_ref[...] = (acc_ref[...] + pv) * alpha

    for c in range(FLASH_AHEAD):
        scores(c, c)
    last = n - blocks
    if last == 0:
        ring_pass(0, blocks - FLASH_AHEAD, True)
    else:
        ring_pass(0, blocks, True)

        def body(i, carry):
            ring_pass(i * blocks, blocks, False)
            return carry

        lax.fori_loop(1, last // blocks, body, 0)
        ring_pass(last, blocks - FLASH_AHEAD, False)
    acc = acc_ref[...] + _dot(v_t_ref[n - 1], p_refs[(n - 1) % 2][...])
    o_ref[...] = (acc[:V_HEAD] / acc[V_HEAD:V_HEAD + 1]).T.astype(o_ref.dtype)


def _flash(q_t, k, v_t, tq):
    b, s, _ = k.shape
    n, sub = v_t.shape[2], v_t.shape[4]
    assert n % FLASH_RING == 0 and n % min(FLASH_PASS, n) == 0 and n * sub == s
    return pl.pallas_call(
        _flash_kernel,
        grid=(b, MLA_HEADS, s // tq),
        in_specs=[pl.BlockSpec((None, QK_PAD, tq), lambda i, h, qi: (i, h, qi)),
                  pl.BlockSpec((None, s, QK_PAD), lambda i, h, qi: (i, 0, h)),
                  pl.BlockSpec((None, None, n, V_AUG, sub), lambda i, h, qi: (i, h, 0, 0, 0))],
        out_specs=pl.BlockSpec((None, tq, V_HEAD), lambda i, h, qi: (i, qi, h)),
        out_shape=jax.ShapeDtypeStruct((b, s, MLA_DIM), BF16),
        scratch_shapes=[pltpu.VMEM((sub, tq), F32)] * FLASH_RING
                       + [pltpu.VMEM((sub, tq), BF16)] * 2
                       + [pltpu.VMEM((FLASH_RING, F32_ROWS, tq), F32),
                          pltpu.VMEM((1, tq), F32), pltpu.VMEM((V_AUG, tq), F32)],
        compiler_params=_params("parallel", "parallel", "arbitrary"),
    )(q_t, k, v_t)


CONV_HALO = 16
CONV_PAD = 8


def _conv_kernel(cur_ref, prev_ref, next_ref, w_ref, b_ref, xs_ref, bm_ref, cm_ref, ext_ref):
    i = pl.program_id(1)
    tc = cur_ref.shape[0]
    prev = prev_ref[...].astype(F32)[CONV_HALO - CONV_PAD:]
    nxt = next_ref[...].astype(F32)[:CONV_PAD]
    ext_ref[0:CONV_PAD, :] = jnp.where(i > 0, prev, 0.0)
    ext_ref[CONV_PAD:CONV_PAD + tc, :] = cur_ref[...].astype(F32)
    ext_ref[CONV_PAD + tc:, :] = jnp.where(i < pl.num_programs(1) - 1, nxt, 0.0)
    ext = ext_ref[...]
    rows = ext.shape[0]
    acc = jnp.broadcast_to(b_ref[...], (tc, CONV_DIM))
    for k in range(CONV_K):
        shifted = ext if k == CONV_K // 2 else pltpu.roll(ext, (CONV_K // 2 - k) % rows, axis=0)
        acc = acc + w_ref[k:k + 1, :] * shifted[CONV_PAD:CONV_PAD + tc]
    y = acc / (1.0 + jnp.exp(-acc))
    xs_ref[...] = y[:, :SSD_DIM].astype(BF16)
    bm_ref[...] = y[:, SSD_DIM:SSD_DIM + BC_DIM].astype(BF16)
    cm_ref[...] = y[:, SSD_DIM + BC_DIM:].astype(BF16)


def _conv(xbc, w, bias, tc):
    b, s, _ = xbc.shape
    per = tc // CONV_HALO
    last = s // CONV_HALO - 1
    return pl.pallas_call(
        _conv_kernel,
        grid=(b, s // tc),
        in_specs=[pl.BlockSpec((None, tc, CONV_DIM), lambda i, j: (i, j, 0)),
                  pl.BlockSpec((None, CONV_HALO, CONV_DIM),
                               lambda i, j: (i, jnp.maximum(j * per - 1, 0), 0)),
                  pl.BlockSpec((None, CONV_HALO, CONV_DIM),
                               lambda i, j: (i, jnp.minimum((j + 1) * per, last), 0)),
                  _resident((8, CONV_DIM)),
                  _resident((1, CONV_DIM))],
        out_specs=[pl.BlockSpec((None, tc, SSD_DIM), lambda i, j: (i, j, 0)),
                   pl.BlockSpec((None, tc, BC_DIM), lambda i, j: (i, j, 0)),
                   pl.BlockSpec((None, tc, BC_DIM), lambda i, j: (i, j, 0))],
        out_shape=[jax.ShapeDtypeStruct((b, s, SSD_DIM), BF16),
                   jax.ShapeDtypeStruct((b, s, BC_DIM), BF16),
                   jax.ShapeDtypeStruct((b, s, BC_DIM), BF16)],
        scratch_shapes=[pltpu.VMEM((tc + 2 * CONV_PAD, CONV_DIM), F32)],
        compiler_params=_params("parallel", "parallel"),
    )(xbc, xbc, xbc, w, bias)


class _SsdChain:
    def __init__(self, direction, r0, xs_ref, bm_ref, cm_ref, small_ref, y_ref):
        self.d, self.r0 = direction, r0
        self.rows = slice(r0, r0 + CHUNK)
        self.xs_ref, self.bm_ref, self.cm_ref, self.small_ref, self.y_ref = (
            xs_ref, bm_ref, cm_ref, small_ref, y_ref)

    def step_size(self, sel_ref, bias_ref):
        raw = _dot_exact_rhs01(self.small_ref[self.rows, :], sel_ref[self.d]) + bias_ref[self.d]
        self.dt = jnp.maximum(raw, 0.0) + jnp.log1p(jnp.exp(-jnp.abs(raw)))

    def decay_log(self, tri_ref, a_heads):
        self.g = _dot_split2_lhs01(tri_ref[self.d], self.dt * a_heads[self.d])

    def widen(self, e64_ref):
        self.g_t = self.g.T
        self.g_wide = _dot_split2_rhs01(self.g, e64_ref[...])
        self.dt_wide = _dot_split2_rhs01(self.dt, e64_ref[...])

    def within_chunk(self):
        reverse = self.d == 1
        row = lax.broadcasted_iota(jnp.int32, (CHUNK, CHUNK), 0)
        col = lax.broadcasted_iota(jnp.int32, (CHUNK, CHUNK), 1)
        mask = (row <= col) if reverse else (row >= col)
        end = 0 if reverse else CHUNK - 1
        g, g_t, g_wide = self.g, self.g_t, self.g_wide
        self.g_end = g_wide[end:end + 1, :]
        xd = self.xs_ref[self.rows, :].astype(F32) * self.dt_wide
        first = lax.broadcasted_iota(jnp.int32, (CHUNK, SSD_DIM), 1) % LANES < SSD_HEAD_DIM
        xd_first = jnp.where(first, xd, 0.0).astype(BF16)
        xd_second = jnp.where(first, 0.0, xd).astype(BF16)
        self.xdw = (xd * jnp.exp(self.g_end - g_wide)).astype(BF16)
        self.scale_off = jnp.exp(g_wide)
        bm = self.bm_ref[self.rows, :]
        cm = self.cm_ref[self.rows, :]
        self.b_g, self.c_g, self.y_diag = [], [], []
        for grp in range(SSD_GROUPS):
            n0 = grp * SSD_STATE
            b_g = bm[:, n0:n0 + SSD_STATE]
            c_g = cm[:, n0:n0 + SSD_STATE]
            self.b_g.append(b_g)
            self.c_g.append(c_g)
            cb = _dot_nt(c_g, b_g)
            for pr in range(SSD_HPG // 2):
                h0 = grp * SSD_HPG + 2 * pr
                ws = []
                for h in (h0, h0 + 1):
                    diff = g[:, h:h + 1] - g_t[h:h + 1, :]
                    ws.append((cb * jnp.exp(jnp.where(mask, diff, -jnp.inf))).astype(BF16))
                p0 = h0 * SSD_HEAD_DIM
                rhs = jnp.concatenate([xd_first[:, p0:p0 + LANES], xd_second[:, p0:p0 + LANES]],
                                      axis=0)
                self.y_diag.append(_dot(jnp.concatenate(ws, axis=1), rhs))

    def across_chunks(self, state_ref):
        for grp in range(SSD_GROUPS):
            d0 = grp * SSD_GROUP_DIM
            state = state_ref[self.d, grp]
            y_off = (_dot(self.c_g[grp], state.astype(BF16))
                     * self.scale_off[:, d0:d0 + SSD_GROUP_DIM])
            for pr in range(SSD_HPG // 2):
                p0 = d0 + pr * LANES
                y = self.y_diag[grp * (SSD_HPG // 2) + pr] + y_off[:, pr * LANES:(pr + 1) * LANES]
                self.y_ref[self.rows, p0:p0 + LANES] = y.astype(self.y_ref.dtype)
            state_ref[self.d, grp] = (state * jnp.exp(self.g_end[:, d0:d0 + SSD_GROUP_DIM])
                                      + _dot_tn(self.b_g[grp], self.xdw[:, d0:d0 + SSD_GROUP_DIM]))


def _ssd_kernel(xs_f_ref, bm_f_ref, cm_f_ref, small_f_ref, xs_b_ref, bm_b_ref, cm_b_ref, small_b_ref,
                sel_ref, e64_ref, tri_ref, bias_ref, alog_ref, yf_ref, yb_ref, state_ref, *, chunks):
    @pl.when(pl.program_id(1) == 0)
    def _():
        state_ref[...] = jnp.zeros(state_ref.shape, F32)

    a_heads = -jnp.exp(alog_ref[...])
    chains = []
    for c in range(chunks):
        chains.append(_SsdChain(0, c * CHUNK, xs_f_ref, bm_f_ref, cm_f_ref, small_f_ref, yf_ref))
        chains.append(_SsdChain(1, (chunks - 1 - c) * CHUNK, xs_b_ref, bm_b_ref, cm_b_ref,
                                small_b_ref, yb_ref))
    for ch in chains:
        ch.step_size(sel_ref, bias_ref)
    for ch in chains:
        ch.decay_log(tri_ref, a_heads)
    for ch in chains:
        ch.widen(e64_ref)
    for ch in chains:
        ch.within_chunk()
    for ch in chains:
        ch.across_chunks(state_ref)


def _ssd_consts():
    sel = np.zeros((2, LANES, LANES), np.float32)
    for d, base in enumerate((SMALL_DTF, SMALL_DTB)):
        sel[d, base + np.arange(SSD_HEADS), np.arange(SSD_HEADS)] = 1.0
    e64 = np.zeros((LANES, SSD_DIM), np.float32)
    for h in range(SSD_HEADS):
        e64[h, h * SSD_HEAD_DIM:(h + 1) * SSD_HEAD_DIM] = 1.0
    lower = np.tril(np.ones((CHUNK, CHUNK), np.float32))
    tri = np.stack([lower, lower.T])
    return jnp.asarray(sel, BF16), jnp.asarray(e64, BF16), jnp.asarray(tri, BF16)


def _ssd(xs, bm, cm, small, dt_bias_f, dt_bias_b, a_log_f, a_log_b, chunks):
    b, s, _ = xs.shape
    rows = chunks * CHUNK
    n = s // rows
    sel, e64, tri = _ssd_consts()
    pad = lambda f, bk: jnp.pad(jnp.stack([f, bk]).astype(F32),
                                ((0, 0), (0, LANES - SSD_HEADS))).reshape(2, 1, LANES)
    fwd = lambda i, j: (i, j, 0)
    bwd = lambda i, j: (i, n - 1 - j, 0)
    tiles = lambda tok: [pl.BlockSpec((None, rows, SSD_DIM), tok),
                         pl.BlockSpec((None, rows, BC_DIM), tok),
                         pl.BlockSpec((None, rows, BC_DIM), tok),
                         pl.BlockSpec((None, rows, SMALL), tok)]
    return pl.pallas_call(
        functools.partial(_ssd_kernel, chunks=chunks),
        grid=(b, n),
        in_specs=tiles(fwd) + tiles(bwd) + [
            _resident((2, LANES, LANES)),
            _resident((LANES, SSD_DIM)),
            _resident((2, CHUNK, CHUNK)),
            _resident((2, 1, LANES)),
            _resident((2, 1, LANES))],
        out_specs=[pl.BlockSpec((None, rows, SSD_DIM), fwd), pl.BlockSpec((None, rows, SSD_DIM), bwd)],
        out_shape=[jax.ShapeDtypeStruct((b, s, SSD_DIM), BF16)] * 2,
        scratch_shapes=[pltpu.VMEM((2, SSD_GROUPS, SSD_STATE, SSD_GROUP_DIM), F32)],
        compiler_params=_params("parallel", "arbitrary"),
    )(xs, bm, cm, small, xs, bm, cm, small, sel, e64, tri,
      pad(dt_bias_f, dt_bias_b), pad(a_log_f, a_log_b))


def _outproj_kernel(x_ref, fo_ref, mo_ref, yf_ref, yb_ref, xs_ref, z_ref, dskip_ref, ng_ref,
                    w_ref, pg_ref, o_ref):
    y = (yf_ref[...].astype(F32) + yb_ref[...].astype(F32)
         + xs_ref[...].astype(F32) * dskip_ref[...])
    z = z_ref[...].astype(F32)
    y = y * (z / (1.0 + jnp.exp(-z)))
    parts = []
    for grp in range(SSD_GROUPS):
        yg = y[:, grp * SSD_GROUP_DIM:(grp + 1) * SSD_GROUP_DIM]
        parts.append(yg * _rms_scale(yg))
    y = (jnp.concatenate(parts, axis=1) * ng_ref[...]).astype(BF16)
    mix = (_dot(fo_ref[...], w_ref[0:FNET_DIM, :])
           + _dot(mo_ref[...], w_ref[FNET_DIM:FNET_DIM + MLA_DIM, :])
           + _dot(y, w_ref[FNET_DIM + MLA_DIM:, :]))
    o_ref[...] = x_ref[...] + mix * _rms_scale(mix) * pg_ref[...]


def _outproj(x, fo, mo, yf, yb, xs, z, dskip, ng, w, pg, tm):
    t = x.shape[0]
    rows = lambda wd: pl.BlockSpec((tm, wd), lambda i: (i, 0))
    return pl.pallas_call(
        _outproj_kernel,
        grid=(t // tm,),
        in_specs=[rows(D_MODEL), rows(FNET_DIM), rows(MLA_DIM), rows(SSD_DIM), rows(SSD_DIM),
                  rows(SSD_DIM), rows(SSD_DIM), _resident((1, SSD_DIM)), _resident((1, SSD_DIM)),
                  _resident((D_MODEL, D_MODEL)), _resident((1, D_MODEL))],
        out_specs=rows(D_MODEL),
        out_shape=jax.ShapeDtypeStruct((t, D_MODEL), F32),
        compiler_params=_params("parallel"),
    )(x, fo, mo, yf, yb, xs, z, dskip, ng, w, pg)


FFN_SUB = 1024


def _ffn_kernel(x_ref, g_ref, w1_ref, w2_ref, pg_ref, o_ref, h_ref):
    j = pl.program_id(1)

    @pl.when(j == 0)
    def _():
        x = x_ref[...]
        h_ref[...] = (x * _rms_scale(x) * g_ref[...]).astype(BF16)
        o_ref[...] = jnp.zeros(o_ref.shape, F32)

    for c0 in range(0, w1_ref.shape[1], FFN_SUB):
        a = jnp.maximum(_dot(h_ref[...], w1_ref[:, c0:c0 + FFN_SUB]), 0.0)
        o_ref[...] += _dot((a * a).astype(BF16), w2_ref[c0:c0 + FFN_SUB, :])

    @pl.when(j == pl.num_programs(1) - 1)
    def _():
        f = o_ref[...]
        o_ref[...] = x_ref[...] + f * _rms_scale(f) * pg_ref[...]


def _ffn(x, g, w1, w2, pg, layer, tm, tf):
    t = x.shape[0]
    return pl.pallas_call(
        _ffn_kernel,
        grid=(t // tm, D_FF // tf),
        in_specs=[pl.BlockSpec((tm, D_MODEL), lambda i, j: (i, 0)),
                  _resident((1, D_MODEL)),
                  pl.BlockSpec((None, D_MODEL, tf), lambda i, j: (layer, 0, j)),
                  pl.BlockSpec((None, tf, D_MODEL), lambda i, j: (layer, j, 0)),
                  _resident((1, D_MODEL))],
        out_specs=pl.BlockSpec((tm, D_MODEL), lambda i, j: (i, 0)),
        out_shape=jax.ShapeDtypeStruct((t, D_MODEL), F32),
        scratch_shapes=[pltpu.VMEM((tm, D_MODEL), BF16)],
        compiler_params=_params("parallel", "arbitrary"),
    )(x, g, w1, w2, pg)


def _prep_layer(pre_mix_g, w_in, q_norm_g, w_q_up, kv_norm_g, w_kv_up, conv_w, conv_b,
                dt_bias_f, dt_bias_b, a_log_f, a_log_b, d_skip, ssd_norm_g, w_out,
                post_mix_g, pre_ffn_g, w_ff1, w_ff2, post_ffn_g):
    o_cq = FNET_DIM
    o_ckv = o_cq + Q_LORA
    o_pe = o_ckv + KV_LORA
    o_z = o_pe + QK_ROPE
    o_xbc = o_z + SSD_DIM
    o_dt = o_xbc + CONV_DIM
    zeros = lambda n: jnp.zeros((D_MODEL, n), w_in.dtype)
    small_cols = jnp.concatenate([
        w_in[:, o_pe:o_pe + ROPE_HALF],
        w_in[:, o_dt:o_dt + 2 * SSD_HEADS],
        zeros(SMALL_PE2 - ROPE_HALF - 2 * SSD_HEADS),
        w_in[:, o_pe + ROPE_HALF:o_pe + QK_ROPE],
        zeros(LANES - SMALL_PE2 - ROPE_HALF)], axis=1)
    w_in_r = jnp.concatenate([w_in[:, :o_pe], w_in[:, o_z:o_dt], small_cols], axis=1).astype(BF16)

    wq = w_q_up.reshape(Q_LORA, MLA_HEADS, QK_HEAD)
    zq = jnp.zeros((Q_LORA, MLA_HEADS, ROPE_HALF), w_q_up.dtype)
    wq = jnp.concatenate([wq[..., :QK_NOPE + ROPE_HALF], zq, wq[..., QK_NOPE + ROPE_HALF:], zq], axis=-1)
    wq_t = wq.reshape(Q_LORA, MLA_HEADS * QK_PAD).T.astype(BF16)
    wkv = w_kv_up.reshape(KV_LORA, MLA_HEADS, QK_NOPE + V_HEAD)
    wk = wkv[..., :QK_NOPE].reshape(KV_LORA, MLA_HEADS * QK_NOPE).astype(BF16)
    wv_t = wkv[..., QK_NOPE:].reshape(KV_LORA, MLA_DIM).T.astype(BF16)

    row = lambda v: v.astype(F32).reshape(1, -1)
    return dict(
        pre_mix_g=row(pre_mix_g), w_in=w_in_r, q_norm_g=row(q_norm_g), kv_norm_g=row(kv_norm_g),
        wq_t=wq_t, wk=wk, wv_t=wv_t,
        conv_w=jnp.pad(conv_w.astype(F32), ((0, 8 - CONV_K), (0, 0))), conv_b=row(conv_b),
        dt_bias_f=dt_bias_f, dt_bias_b=dt_bias_b, a_log_f=a_log_f, a_log_b=a_log_b,
        d_skip=row(jnp.repeat(d_skip, SSD_HEAD_DIM)), ssd_norm_g=row(ssd_norm_g),
        w_out=w_out.astype(BF16), post_mix_g=row(post_mix_g), pre_ffn_g=row(pre_ffn_g),
        post_ffn_g=row(post_ffn_g))


def _tile(n, want):
    return want if n % want == 0 else n


def _layer(x, p):
    b, s, _ = x.shape
    t = b * s
    xt = x.reshape(t, D_MODEL)
    uf, cq, ckv, z, xbc, small = _inproj(xt, p["pre_mix_g"], p["w_in"], _tile(t, 512))
    small3 = small.reshape(b, s, SMALL)

    fo = _fourier(uf.reshape(b, s, FNET_DIM)).reshape(t, FNET_DIM)

    q_t, k, v_t = _mla_prep(cq.reshape(b, s, Q_LORA), ckv.reshape(b, s, KV_LORA), small3,
                            p["q_norm_g"], p["kv_norm_g"], p["wq_t"], p["wk"], p["wv_t"],
                            _tile(s, 1024) if s >= 4096 else _tile(s, 512))
    mo = _flash(q_t, k, v_t, _tile(s, 512) if s >= 4096 else _tile(s, 1024)).reshape(t, MLA_DIM)

    xs, bm, cm = _conv(xbc.reshape(b, s, CONV_DIM), p["conv_w"], p["conv_b"], _tile(s, 512))
    chunks = 8 if s % (8 * CHUNK) == 0 else 1
    yf, yb = _ssd(xs, bm, cm, small3, p["dt_bias_f"], p["dt_bias_b"], p["a_log_f"], p["a_log_b"],
                  chunks)

    x1 = _outproj(xt, fo, mo, yf.reshape(t, SSD_DIM), yb.reshape(t, SSD_DIM),
                  xs.reshape(t, SSD_DIM), z, p["d_skip"], p["ssd_norm_g"], p["w_out"],
                  p["post_mix_g"], _tile(t, 512))
    x2 = _ffn(x1, p["pre_ffn_g"], p["w_ff1"], p["w_ff2"], p["post_ffn_g"], p["layer"],
              _tile(t, 512), 2048)
    return x2.reshape(b, s, D_MODEL)


def kernel(x_prompt, x_sample, pre_mix_g, w_in, q_norm_g, w_q_up, kv_norm_g, w_kv_up, conv_w, conv_b, dt_bias_f, dt_bias_b, a_log_f, a_log_b, d_skip, ssd_norm_g, w_out, post_mix_g, pre_ffn_g, w_ff1, w_ff2, post_ffn_g):
    weights = (pre_mix_g, w_in, q_norm_g, w_q_up, kv_norm_g, w_kv_up, conv_w, conv_b,
               dt_bias_f, dt_bias_b, a_log_f, a_log_b, d_skip, ssd_norm_g, w_out,
               post_mix_g, pre_ffn_g, w_ff1, w_ff2, post_ffn_g)
    depth = w_in.shape[0]
    layers = [_prep_layer(*[w[l] for w in weights]) for l in range(depth)]
    w_ff1_b, w_ff2_b = w_ff1.astype(BF16), w_ff2.astype(BF16)
    for l, p in enumerate(layers):
        p.update(layer=l, w_ff1=w_ff1_b, w_ff2=w_ff2_b)

    def run_trunk(x):
        for p in layers:
            x = _layer(x, p)
        return x

    return (run_trunk(x_prompt), run_trunk(x_sample))
```

```python
import functools
import math

import jax
import jax.numpy as jnp
import numpy as np
from jax import lax
from jax.experimental import pallas as pl
from jax.experimental.pallas import tpu as pltpu

F32 = jnp.float32
BF16 = jnp.bfloat16

D_MODEL = 2048
FNET_GROUPS = 4
FNET_GROUP_DIM = 128
FNET_DIM = FNET_GROUPS * FNET_GROUP_DIM
MLA_HEADS = 6
Q_LORA = 512
KV_LORA = 512
QK_NOPE = 128
QK_ROPE = 64
ROPE_HALF = QK_ROPE // 2
V_HEAD = 128
V_AUG = V_HEAD + 16
QK_HEAD = QK_NOPE + QK_ROPE
QK_PAD = 256
MLA_DIM = MLA_HEADS * V_HEAD
ROPE_THETA = 10000.0
SSD_HEADS = 12
SSD_HEAD_DIM = 64
SSD_DIM = SSD_HEADS * SSD_HEAD_DIM
SSD_GROUPS = 2
SSD_HPG = SSD_HEADS // SSD_GROUPS
SSD_GROUP_DIM = SSD_DIM // SSD_GROUPS
SSD_STATE = 128
CONV_K = 5
CHUNK = 128
CONV_DIM = SSD_DIM + 2 * SSD_GROUPS * SSD_STATE
BC_DIM = SSD_GROUPS * SSD_STATE
D_FF = 4 * D_MODEL
EPS = 1e-6

LANES = 128
F32_ROWS = 8
BF16_ROWS = 16
SMALL = LANES
SMALL_DTF = ROPE_HALF
SMALL_DTB = ROPE_HALF + SSD_HEADS
SMALL_PE2 = LANES // 2
PROJ_DIM = FNET_DIM + Q_LORA + KV_LORA + SSD_DIM + CONV_DIM + SMALL

VMEM_LIMIT = 60 * 1024 * 1024
FNET_S2_TILE = 8
FNET_K1_TILE = 8


def _params(*sem):
    return pltpu.CompilerParams(dimension_semantics=sem, vmem_limit_bytes=VMEM_LIMIT)


def _dot(a, b):
    return jnp.dot(a, b, preferred_element_type=F32)


def _dot_nt(a, b):
    return lax.dot_general(a, b, (((1,), (1,)), ((), ())), preferred_element_type=F32)


def _dot_tn(a, b):
    return lax.dot_general(a, b, (((0,), (0,)), ((), ())), preferred_element_type=F32)


def _rms_scale(x):
    return lax.rsqrt(jnp.mean(x * x, axis=-1, keepdims=True) + EPS)


def _split3(x):
    hi = x.astype(BF16)
    r = x - hi.astype(F32)
    mid = r.astype(BF16)
    lo = (r - mid.astype(F32)).astype(BF16)
    return hi, mid, lo


def _dot_exact_rhs01(x, m01):
    hi, mid, lo = _split3(x)
    return _dot(hi, m01) + _dot(mid, m01) + _dot(lo, m01)


def _split2(x):
    hi = x.astype(BF16)
    return hi, (x - hi.astype(F32)).astype(BF16)


def _dot_split2_rhs01(x, m01):
    hi, lo = _split2(x)
    return _dot(hi, m01) + _dot(lo, m01)


def _dot_split2_lhs01(m01, x):
    hi, lo = _split2(x)
    return _dot(m01, hi) + _dot(m01, lo)


def _resident(shape):
    nd = len(shape)
    return pl.BlockSpec(shape, lambda *_: (0,) * nd, pipeline_mode=pl.Buffered(1))


_IN_WIDTHS = (FNET_DIM, Q_LORA, KV_LORA, SSD_DIM, CONV_DIM, SMALL)


def _inproj_kernel(x_ref, g_ref, w_ref, uf_ref, cq_ref, ckv_ref, z_ref, xbc_ref, small_ref):
    x = x_ref[...]
    h = (x * _rms_scale(x) * g_ref[...]).astype(BF16)
    off = 0
    for ref, width in zip((uf_ref, cq_ref, ckv_ref, z_ref, xbc_ref, small_ref), _IN_WIDTHS):
        for c0 in range(0, width, 512):
            cw = min(512, width - c0)
            ref[:, c0:c0 + cw] = _dot(h, w_ref[:, off + c0:off + c0 + cw]).astype(ref.dtype)
        off += width


def _inproj(x, g, w, tm):
    t = x.shape[0]
    dtypes = (BF16, BF16, BF16, BF16, BF16, F32)
    return pl.pallas_call(
        _inproj_kernel,
        grid=(t // tm,),
        in_specs=[pl.BlockSpec((tm, D_MODEL), lambda i: (i, 0)),
                  _resident((1, D_MODEL)),
                  _resident((D_MODEL, PROJ_DIM))],
        out_specs=[pl.BlockSpec((tm, wd), lambda i: (i, 0)) for wd in _IN_WIDTHS],
        out_shape=[jax.ShapeDtypeStruct((t, wd), dt) for wd, dt in zip(_IN_WIDTHS, dtypes)],
        compiler_params=_params("parallel"),
    )(x, g, w)


def _dft_cos_sin(n):
    k = np.arange(n)
    ang = 2.0 * np.pi * ((k[:, None] * k[None, :]) % n) / n
    return np.cos(ang), np.sin(ang)


def _fnet_stage1_kernel(u_ref, m1_ref, twr_ref, twi_ref, yr_ref, yi_ref):
    n1 = u_ref.shape[0]
    y = _dot(m1_ref[...], u_ref[...])
    for s in range(FNET_S2_TILE):
        tr = twr_ref[:, s * LANES:(s + 1) * LANES]
        ti = twi_ref[:, s * LANES:(s + 1) * LANES]
        for q in range(FNET_DIM // LANES):
            c0 = s * FNET_DIM + q * LANES
            yr = y[:n1, c0:c0 + LANES]
            yi = y[n1:, c0:c0 + LANES]
            yr_ref[:, c0:c0 + LANES] = (yr * tr - yi * ti).astype(BF16)
            yi_ref[:, c0:c0 + LANES] = (yr * ti + yi * tr).astype(BF16)


def _fnet_stage2_kernel(yr_ref, yi_ref, m2_ref, mc_ref, o_ref):
    n2 = yr_ref.shape[1]
    for k1 in range(yr_ref.shape[0]):
        y = jnp.concatenate([yr_ref[k1], yi_ref[k1]], axis=0)
        h = _dot(m2_ref[...], y).astype(BF16)
        for g in range(FNET_GROUPS):
            c0 = g * FNET_GROUP_DIM
            hg = jnp.concatenate([h[:n2, c0:c0 + FNET_GROUP_DIM], h[n2:, c0:c0 + FNET_GROUP_DIM]],
                                 axis=1)
            o_ref[k1, :, c0:c0 + FNET_GROUP_DIM] = _dot(hg, mc_ref[...]).astype(BF16)


def _fourier(u):
    b, s, c = u.shape
    n2 = LANES
    n1 = s // n2
    assert n1 * n2 == s and n1 % 16 == 0 and n2 % FNET_S2_TILE == 0
    c1, s1 = _dft_cos_sin(n1)
    m1 = jnp.asarray(np.concatenate([c1, -s1], axis=0), BF16)
    c2, s2 = _dft_cos_sin(n2)
    m2 = jnp.asarray(np.block([[c2, s2], [-s2, c2]]), BF16)
    cc, sc = _dft_cos_sin(FNET_GROUP_DIM)
    mc = jnp.asarray(np.concatenate([cc, sc], axis=0), BF16)
    ang = 2.0 * np.pi * (np.arange(n1)[:, None] * np.arange(n2)[None, :]) / s
    scale = 1.0 / math.sqrt(s * FNET_GROUP_DIM)
    twr = jnp.repeat(jnp.asarray(np.cos(ang) * scale, F32), LANES, axis=1)
    twi = jnp.repeat(jnp.asarray(-np.sin(ang) * scale, F32), LANES, axis=1)

    cols = FNET_S2_TILE * c
    u2 = u.reshape(b, n1, n2 * c)
    yr, yi = pl.pallas_call(
        _fnet_stage1_kernel,
        grid=(b, n2 // FNET_S2_TILE),
        in_specs=[pl.BlockSpec((None, n1, cols), lambda i, j: (i, 0, j)),
                  _resident((2 * n1, n1)),
                  pl.BlockSpec((n1, FNET_S2_TILE * LANES), lambda i, j: (0, j)),
                  pl.BlockSpec((n1, FNET_S2_TILE * LANES), lambda i, j: (0, j))],
        out_specs=[pl.BlockSpec((None, n1, cols), lambda i, j: (i, 0, j))] * 2,
        out_shape=[jax.ShapeDtypeStruct((b, n1, n2 * c), BF16)] * 2,
        compiler_params=_params("parallel", "parallel"),
    )(u2, m1, twr, twi)

    yr = yr.reshape(b, n1, n2, c)
    yi = yi.reshape(b, n1, n2, c)
    out_t = pl.pallas_call(
        _fnet_stage2_kernel,
        grid=(b, n1 // FNET_K1_TILE),
        in_specs=[pl.BlockSpec((None, FNET_K1_TILE, n2, c), lambda i, j: (i, j, 0, 0)),
                  pl.BlockSpec((None, FNET_K1_TILE, n2, c), lambda i, j: (i, j, 0, 0)),
                  _resident((2 * n2, 2 * n2)),
                  _resident((2 * FNET_GROUP_DIM, FNET_GROUP_DIM))],
        out_specs=pl.BlockSpec((None, FNET_K1_TILE, n2, c), lambda i, j: (i, j, 0, 0)),
        out_shape=jax.ShapeDtypeStruct((b, n1, n2, c), BF16),
        compiler_params=_params("parallel", "parallel"),
    )(yr, yi, m2, mc)
    return out_t.transpose(0, 2, 1, 3).reshape(b, s, c)


def _mla_prep_kernel(cq_ref, ckv_ref, small_ref, cos_t_ref, sin_t_ref, cos_k_ref, sin_k_ref,
                     qg_ref, kvg_ref, wq_t_ref, wk_ref, wv_t_ref, q_t_ref, k_ref, v_t_ref):
    cq = cq_ref[...].astype(F32)
    cqn = (cq * _rms_scale(cq) * qg_ref[...]).astype(BF16)
    ckv = ckv_ref[...].astype(F32)
    ckvn = (ckv * _rms_scale(ckv) * kvg_ref[...]).astype(BF16)
    q_scale = math.log2(math.e) / math.sqrt(QK_HEAD)
    cos_t = cos_t_ref[...]
    sin_t = sin_t_ref[...]
    half = LANES // 2
    for h in range(MLA_HEADS):
        r0 = h * QK_PAD
        q_t = _dot_nt(wq_t_ref[r0:r0 + QK_PAD, :], cqn)
        q_t_ref[r0:r0 + QK_NOPE, :] = (q_t[:QK_NOPE] * q_scale).astype(BF16)
        blk = q_t[QK_NOPE:]
        swapped = jnp.concatenate([blk[half:], blk[:half]], axis=0)
        q_t_ref[r0 + QK_NOPE:r0 + QK_PAD, :] = ((blk * cos_t + swapped * sin_t) * q_scale).astype(BF16)

    small = small_ref[...]
    k_rope = (small * cos_k_ref[...] + pltpu.roll(small, half, axis=1) * sin_k_ref[...]).astype(BF16)
    for h0 in range(0, MLA_HEADS, 2):
        k_pair = _dot(ckvn, wk_ref[:, h0 * QK_NOPE:(h0 + 2) * QK_NOPE]).astype(BF16)
        for h in (h0, h0 + 1):
            c0 = h * QK_PAD
            k_ref[:, c0:c0 + QK_NOPE] = k_pair[:, (h - h0) * QK_NOPE:(h - h0 + 1) * QK_NOPE]
            k_ref[:, c0 + QK_NOPE:c0 + QK_PAD] = k_rope
    for h in range(MLA_HEADS):
        v_t_ref[h, :V_HEAD, :] = _dot_nt(wv_t_ref[h * V_HEAD:(h + 1) * V_HEAD, :], ckvn).astype(BF16)
        v_t_ref[h, V_HEAD:, :] = jnp.ones((V_AUG - V_HEAD, v_t_ref.shape[2]), BF16)


def _rope_tables(s):
    pos = jnp.arange(s, dtype=F32)
    inv = ROPE_THETA ** (-jnp.arange(0, QK_ROPE, 2, dtype=F32) / QK_ROPE)
    ang = pos[:, None] * inv[None, :]
    cos, sin = jnp.cos(ang), jnp.sin(ang)
    zero = jnp.zeros_like(cos)
    cos_k = jnp.concatenate([cos, zero, cos, zero], axis=1)
    sin_k = jnp.concatenate([-sin, zero, sin, zero], axis=1)
    return cos_k, sin_k


def _mla_prep(cq, ckv, small, qg, kvg, wq_t, wk, wv_t, tm):
    b, s, _ = cq.shape
    cos_k, sin_k = _rope_tables(s)
    cos_t, sin_t = cos_k.T, sin_k.T
    tok = lambda i, j: (i, j, 0)
    tok_t = lambda i, j: (i, 0, j)
    return pl.pallas_call(
        _mla_prep_kernel,
        grid=(b, s // tm),
        in_specs=[pl.BlockSpec((None, tm, Q_LORA), tok),
                  pl.BlockSpec((None, tm, KV_LORA), tok),
                  pl.BlockSpec((None, tm, SMALL), tok),
                  pl.BlockSpec((LANES, tm), lambda i, j: (0, j)),
                  pl.BlockSpec((LANES, tm), lambda i, j: (0, j)),
                  pl.BlockSpec((tm, LANES), lambda i, j: (j, 0)),
                  pl.BlockSpec((tm, LANES), lambda i, j: (j, 0)),
                  _resident((1, Q_LORA)),
                  _resident((1, KV_LORA)),
                  _resident((MLA_HEADS * QK_PAD, Q_LORA)),
                  _resident((KV_LORA, MLA_HEADS * QK_NOPE)),
                  _resident((MLA_DIM, KV_LORA))],
        out_specs=[pl.BlockSpec((None, MLA_HEADS * QK_PAD, tm), tok_t),
                   pl.BlockSpec((None, tm, MLA_HEADS * QK_PAD), tok),
                   pl.BlockSpec((None, MLA_HEADS, None, V_AUG, tm), lambda i, j: (i, 0, j, 0, 0))],
        out_shape=[jax.ShapeDtypeStruct((b, MLA_HEADS * QK_PAD, s), BF16),
                   jax.ShapeDtypeStruct((b, s, MLA_HEADS * QK_PAD), BF16),
                   jax.ShapeDtypeStruct((b, MLA_HEADS, s // tm, V_AUG, tm), BF16)],
        compiler_params=_params("parallel", "parallel"),
    )(cq, ckv, small, cos_t, sin_t, cos_k, sin_k, qg, kvg, wq_t, wk, wv_t)


FLASH_RING = 4
FLASH_AHEAD = 2
FLASH_PASS = 4


def _flash_kernel(q_t_ref, k_ref, v_t_ref, o_ref, s0_ref, s1_ref, s2_ref, s3_ref, p0_ref, p1_ref,
                  mx_ref, m_ref, acc_ref):
    n, _, sub = v_t_ref.shape
    tq = q_t_ref.shape[1]
    s_refs = (s0_ref, s1_ref, s2_ref, s3_ref)
    p_refs = (p0_ref, p1_ref)
    m_ref[...] = jnp.full(m_ref.shape, -jnp.inf, F32)
    acc_ref[...] = jnp.zeros(acc_ref.shape, F32)

    def scores(c, slot):
        k0 = c * sub if isinstance(c, int) else pl.multiple_of(c * sub, sub)
        s = _dot(k_ref[pl.ds(k0, sub), :], q_t_ref[...])
        s_refs[slot][...] = s
        mx_ref[slot] = jnp.max(s.reshape(sub // F32_ROWS, F32_ROWS, tq), axis=0)

    def softmax(slot, p_ref):
        s_ref = s_refs[slot]
        m_prev = m_ref[...]
        m_new = jnp.maximum(m_prev, jnp.max(mx_ref[slot], axis=0, keepdims=True))
        alpha = jnp.exp2(m_prev - m_new)
        m_rows = jnp.broadcast_to(m_new, (BF16_ROWS, tq))
        for r in range(0, sub, BF16_ROWS):
            p_ref[r:r + BF16_ROWS, :] = jnp.exp2((s_ref[r:r + BF16_ROWS, :] - m_rows).astype(BF16))
        m_ref[...] = m_new
        return alpha

    blocks = min(FLASH_PASS, n)

    def ring_pass(c, lookahead, first):
        for j in range(blocks):
            slot = j % FLASH_RING
            if j < lookahead:
                scores(c + j + FLASH_AHEAD, (slot + FLASH_AHEAD) % FLASH_RING)
            if first and j == 0:
                softmax(0, p_refs[0])
                continue
            pv = _dot(v_t_ref[c + j - 1], p_refs[(j + 1) % 2][...])
            alpha = softmax(slot, p_refs[j % 2])
            acc_ref[...] = (acc_ref[...] + pv) * alpha

    for c in range(FLASH_AHEAD):
        scores(c, c)
    last = n - blocks
    if last == 0:
        ring_pass(0, blocks - FLASH_AHEAD, True)
    else:
        ring_pass(0, blocks, True)

        def body(i, carry):
            ring_pass(i * blocks, blocks, False)
            return carry

        lax.fori_loop(1, last // blocks, body, 0)
        ring_pass(last, blocks - FLASH_AHEAD, False)
    acc = acc_ref[...] + _dot(v_t_ref[n - 1], p_refs[(n - 1) % 2][...])
    o_ref[...] = (acc[:V_HEAD] / acc[V_HEAD:V_HEAD + 1]).T.astype(o_ref.dtype)


def _flash(q_t, k, v_t, tq):
    b, s, _ = k.shape
    n, sub = v_t.shape[2], v_t.shape[4]
    assert n % FLASH_RING == 0 and n % min(FLASH_PASS, n) == 0 and n * sub == s
    return pl.pallas_call(
        _flash_kernel,
        grid=(b, MLA_HEADS, s // tq),
        in_specs=[pl.BlockSpec((None, QK_PAD, tq), lambda i, h, qi: (i, h, qi)),
                  pl.BlockSpec((None, s, QK_PAD), lambda i, h, qi: (i, 0, h)),
                  pl.BlockSpec((None, None, n, V_AUG, sub), lambda i, h, qi: (i, h, 0, 0, 0))],
        out_specs=pl.BlockSpec((None, tq, V_HEAD), lambda i, h, qi: (i, qi, h)),
        out_shape=jax.ShapeDtypeStruct((b, s, MLA_DIM), BF16),
        scratch_shapes=[pltpu.VMEM((sub, tq), F32)] * FLASH_RING
                       + [pltpu.VMEM((sub, tq), BF16)] * 2
                       + [pltpu.VMEM((FLASH_RING, F32_ROWS, tq), F32),
                          pltpu.VMEM((1, tq), F32), pltpu.VMEM((V_AUG, tq), F32)],
        compiler_params=_params("parallel", "parallel", "arbitrary"),
    )(q_t, k, v_t)


CONV_HALO = 16
CONV_PAD = 8


def _conv_kernel(cur_ref, prev_ref, next_ref, w_ref, b_ref, xs_ref, bm_ref, cm_ref, ext_ref):
    i = pl.program_id(1)
    tc = cur_ref.shape[0]
    prev = prev_ref[...].astype(F32)[CONV_HALO - CONV_PAD:]
    nxt = next_ref[...].astype(F32)[:CONV_PAD]
    ext_ref[0:CONV_PAD, :] = jnp.where(i > 0, prev, 0.0)
    ext_ref[CONV_PAD:CONV_PAD + tc, :] = cur_ref[...].astype(F32)
    ext_ref[CONV_PAD + tc:, :] = jnp.where(i < pl.num_programs(1) - 1, nxt, 0.0)
    ext = ext_ref[...]
    rows = ext.shape[0]
    acc = jnp.broadcast_to(b_ref[...], (tc, CONV_DIM))
    for k in range(CONV_K):
        shifted = ext if k == CONV_K // 2 else pltpu.roll(ext, (CONV_K // 2 - k) % rows, axis=0)
        acc = acc + w_ref[k:k + 1, :] * shifted[CONV_PAD:CONV_PAD + tc]
    y = acc / (1.0 + jnp.exp(-acc))
    xs_ref[...] = y[:, :SSD_DIM].astype(BF16)
    bm_ref[...] = y[:, SSD_DIM:SSD_DIM + BC_DIM].astype(BF16)
    cm_ref[...] = y[:, SSD_DIM + BC_DIM:].astype(BF16)


def _conv(xbc, w, bias, tc):
    b, s, _ = xbc.shape
    per = tc // CONV_HALO
    last = s // CONV_HALO - 1
    return pl.pallas_call(
        _conv_kernel,
        grid=(b, s // tc),
        in_specs=[pl.BlockSpec((None, tc, CONV_DIM), lambda i, j: (i, j, 0)),
                  pl.BlockSpec((None, CONV_HALO, CONV_DIM),
                               lambda i, j: (i, jnp.maximum(j * per - 1, 0), 0)),
                  pl.BlockSpec((None, CONV_HALO, CONV_DIM),
                               lambda i, j: (i, jnp.minimum((j + 1) * per, last), 0)),
                  _resident((8, CONV_DIM)),
                  _resident((1, CONV_DIM))],
        out_specs=[pl.BlockSpec((None, tc, SSD_DIM), lambda i, j: (i, j, 0)),
                   pl.BlockSpec((None, tc, BC_DIM), lambda i, j: (i, j, 0)),
                   pl.BlockSpec((None, tc, BC_DIM), lambda i, j: (i, j, 0))],
        out_shape=[jax.ShapeDtypeStruct((b, s, SSD_DIM), BF16),
                   jax.ShapeDtypeStruct((b, s, BC_DIM), BF16),
                   jax.ShapeDtypeStruct((b, s, BC_DIM), BF16)],
        scratch_shapes=[pltpu.VMEM((tc + 2 * CONV_PAD, CONV_DIM), F32)],
        compiler_params=_params("parallel", "parallel"),
    )(xbc, xbc, xbc, w, bias)


class _SsdChain:
    def __init__(self, direction, r0, xs_ref, bm_ref, cm_ref, small_ref, y_ref):
        self.d, self.r0 = direction, r0
        self.rows = slice(r0, r0 + CHUNK)
        self.xs_ref, self.bm_ref, self.cm_ref, self.small_ref, self.y_ref = (
            xs_ref, bm_ref, cm_ref, small_ref, y_ref)

    def step_size(self, sel_ref, bias_ref):
        raw = _dot_exact_rhs01(self.small_ref[self.rows, :], sel_ref[self.d]) + bias_ref[self.d]
        self.dt = jnp.maximum(raw, 0.0) + jnp.log1p(jnp.exp(-jnp.abs(raw)))

    def decay_log(self, tri_ref, a_heads):
        self.g = _dot_split2_lhs01(tri_ref[self.d], self.dt * a_heads[self.d])

    def widen(self, e64_ref):
        self.g_t = self.g.T
        self.g_wide = _dot_split2_rhs01(self.g, e64_ref[...])
        self.dt_wide = _dot_split2_rhs01(self.dt, e64_ref[...])

    def within_chunk(self):
        reverse = self.d == 1
        row = lax.broadcasted_iota(jnp.int32, (CHUNK, CHUNK), 0)
        col = lax.broadcasted_iota(jnp.int32, (CHUNK, CHUNK), 1)
        mask = (row <= col) if reverse else (row >= col)
        end = 0 if reverse else CHUNK - 1
        g, g_t, g_wide = self.g, self.g_t, self.g_wide
        self.g_end = g_wide[end:end + 1, :]
        xd = self.xs_ref[self.rows, :].astype(F32) * self.dt_wide
        first = lax.broadcasted_iota(jnp.int32, (CHUNK, SSD_DIM), 1) % LANES < SSD_HEAD_DIM
        xd_first = jnp.where(first, xd, 0.0).astype(BF16)
        xd_second = jnp.where(first, 0.0, xd).astype(BF16)
        self.xdw = (xd * jnp.exp(self.g_end - g_wide)).astype(BF16)
        self.scale_off = jnp.exp(g_wide)
        bm = self.bm_ref[self.rows, :]
        cm = self.cm_ref[self.rows, :]
        self.b_g, self.c_g, self.y_diag = [], [], []
        for grp in range(SSD_GROUPS):
            n0 = grp * SSD_STATE
            b_g = bm[:, n0:n0 + SSD_STATE]
            c_g = cm[:, n0:n0 + SSD_STATE]
            self.b_g.append(b_g)
            self.c_g.append(c_g)
            cb = _dot_nt(c_g, b_g)
            for pr in range(SSD_HPG // 2):
                h0 = grp * SSD_HPG + 2 * pr
                ws = []
                for h in (h0, h0 + 1):
                    diff = g[:, h:h + 1] - g_t[h:h + 1, :]
                    ws.append((cb * jnp.exp(jnp.where(mask, diff, -jnp.inf))).astype(BF16))
                p0 = h0 * SSD_HEAD_DIM
                rhs = jnp.concatenate([xd_first[:, p0:p0 + LANES], xd_second[:, p0:p0 + LANES]],
                                      axis=0)
                self.y_diag.append(_dot(jnp.concatenate(ws, axis=1), rhs))

    def across_chunks(self, state_ref):
        for grp in range(SSD_GROUPS):
            d0 = grp * SSD_GROUP_DIM
            state = state_ref[self.d, grp]
            y_off = (_dot(self.c_g[grp], state.astype(BF16))
                     * self.scale_off[:, d0:d0 + SSD_GROUP_DIM])
            for pr in range(SSD_HPG // 2):
                p0 = d0 + pr * LANES
                y = self.y_diag[grp * (SSD_HPG // 2) + pr] + y_off[:, pr * LANES:(pr + 1) * LANES]
                self.y_ref[self.rows, p0:p0 + LANES] = y.astype(self.y_ref.dtype)
            state_ref[self.d, grp] = (state * jnp.exp(self.g_end[:, d0:d0 + SSD_GROUP_DIM])
                                      + _dot_tn(self.b_g[grp], self.xdw[:, d0:d0 + SSD_GROUP_DIM]))


def _ssd_kernel(xs_f_ref, bm_f_ref, cm_f_ref, small_f_ref, xs_b_ref, bm_b_ref, cm_b_ref, small_b_ref,
                sel_ref, e64_ref, tri_ref, bias_ref, alog_ref, yf_ref, yb_ref, state_ref, *, chunks):
    @pl.when(pl.program_id(1) == 0)
    def _():
        state_ref[...] = jnp.zeros(state_ref.shape, F32)

    a_heads = -jnp.exp(alog_ref[...])
    chains = []
    for c in range(chunks):
        chains.append(_SsdChain(0, c * CHUNK, xs_f_ref, bm_f_ref, cm_f_ref, small_f_ref, yf_ref))
        chains.append(_SsdChain(1, (chunks - 1 - c) * CHUNK, xs_b_ref, bm_b_ref, cm_b_ref,
                                small_b_ref, yb_ref))
    for ch in chains:
        ch.step_size(sel_ref, bias_ref)
    for ch in chains:
        ch.decay_log(tri_ref, a_heads)
    for ch in chains:
        ch.widen(e64_ref)
    for ch in chains:
        ch.within_chunk()
    for ch in chains:
        ch.across_chunks(state_ref)


def _ssd_consts():
    sel = np.zeros((2, LANES, LANES), np.float32)
    for d, base in enumerate((SMALL_DTF, SMALL_DTB)):
        sel[d, base + np.arange(SSD_HEADS), np.arange(SSD_HEADS)] = 1.0
    e64 = np.zeros((LANES, SSD_DIM), np.float32)
    for h in range(SSD_HEADS):
        e64[h, h * SSD_HEAD_DIM:(h + 1) * SSD_HEAD_DIM] = 1.0
    lower = np.tril(np.ones((CHUNK, CHUNK), np.float32))
    tri = np.stack([lower, lower.T])
    return jnp.asarray(sel, BF16), jnp.asarray(e64, BF16), jnp.asarray(tri, BF16)


def _ssd(xs, bm, cm, small, dt_bias_f, dt_bias_b, a_log_f, a_log_b, chunks):
    b, s, _ = xs.shape
    rows = chunks * CHUNK
    n = s // rows
    sel, e64, tri = _ssd_consts()
    pad = lambda f, bk: jnp.pad(jnp.stack([f, bk]).astype(F32),
                                ((0, 0), (0, LANES - SSD_HEADS))).reshape(2, 1, LANES)
    fwd = lambda i, j: (i, j, 0)
    bwd = lambda i, j: (i, n - 1 - j, 0)
    tiles = lambda tok: [pl.BlockSpec((None, rows, SSD_DIM), tok),
                         pl.BlockSpec((None, rows, BC_DIM), tok),
                         pl.BlockSpec((None, rows, BC_DIM), tok),
                         pl.BlockSpec((None, rows, SMALL), tok)]
    return pl.pallas_call(
        functools.partial(_ssd_kernel, chunks=chunks),
        grid=(b, n),
        in_specs=tiles(fwd) + tiles(bwd) + [
            _resident((2, LANES, LANES)),
            _resident((LANES, SSD_DIM)),
            _resident((2, CHUNK, CHUNK)),
            _resident((2, 1, LANES)),
            _resident((2, 1, LANES))],
        out_specs=[pl.BlockSpec((None, rows, SSD_DIM), fwd), pl.BlockSpec((None, rows, SSD_DIM), bwd)],
        out_shape=[jax.ShapeDtypeStruct((b, s, SSD_DIM), BF16)] * 2,
        scratch_shapes=[pltpu.VMEM((2, SSD_GROUPS, SSD_STATE, SSD_GROUP_DIM), F32)],
        compiler_params=_params("parallel", "arbitrary"),
    )(xs, bm, cm, small, xs, bm, cm, small, sel, e64, tri,
      pad(dt_bias_f, dt_bias_b), pad(a_log_f, a_log_b))


def _outproj_kernel(x_ref, fo_ref, mo_ref, yf_ref, yb_ref, xs_ref, z_ref, dskip_ref, ng_ref,
                    w_ref, pg_ref, o_ref):
    y = (yf_ref[...].astype(F32) + yb_ref[...].astype(F32)
         + xs_ref[...].astype(F32) * dskip_ref[...])
    z = z_ref[...].astype(F32)
    y = y * (z / (1.0 + jnp.exp(-z)))
    parts = []
    for grp in range(SSD_GROUPS):
        yg = y[:, grp * SSD_GROUP_DIM:(grp + 1) * SSD_GROUP_DIM]
        parts.append(yg * _rms_scale(yg))
    y = (jnp.concatenate(parts, axis=1) * ng_ref[...]).astype(BF16)
    mix = (_dot(fo_ref[...], w_ref[0:FNET_DIM, :])
           + _dot(mo_ref[...], w_ref[FNET_DIM:FNET_DIM + MLA_DIM, :])
           + _dot(y, w_ref[FNET_DIM + MLA_DIM:, :]))
    o_ref[...] = x_ref[...] + mix * _rms_scale(mix) * pg_ref[...]


def _outproj(x, fo, mo, yf, yb, xs, z, dskip, ng, w, pg, tm):
    t = x.shape[0]
    rows = lambda wd: pl.BlockSpec((tm, wd), lambda i: (i, 0))
    return pl.pallas_call(
        _outproj_kernel,
        grid=(t // tm,),
        in_specs=[rows(D_MODEL), rows(FNET_DIM), rows(MLA_DIM), rows(SSD_DIM), rows(SSD_DIM),
                  rows(SSD_DIM), rows(SSD_DIM), _resident((1, SSD_DIM)), _resident((1, SSD_DIM)),
                  _resident((D_MODEL, D_MODEL)), _resident((1, D_MODEL))],
        out_specs=rows(D_MODEL),
        out_shape=jax.ShapeDtypeStruct((t, D_MODEL), F32),
        compiler_params=_params("parallel"),
    )(x, fo, mo, yf, yb, xs, z, dskip, ng, w, pg)


FFN_SUB = 1024


def _ffn_kernel(x_ref, g_ref, w1_ref, w2_ref, pg_ref, o_ref, h_ref):
    j = pl.program_id(1)

    @pl.when(j == 0)
    def _():
        x = x_ref[...]
        h_ref[...] = (x * _rms_scale(x) * g_ref[...]).astype(BF16)
        o_ref[...] = jnp.zeros(o_ref.shape, F32)

    for c0 in range(0, w1_ref.shape[1], FFN_SUB):
        a = jnp.maximum(_dot(h_ref[...], w1_ref[:, c0:c0 + FFN_SUB]), 0.0)
        o_ref[...] += _dot((a * a).astype(BF16), w2_ref[c0:c0 + FFN_SUB, :])

    @pl.when(j == pl.num_programs(1) - 1)
    def _():
        f = o_ref[...]
        o_ref[...] = x_ref[...] + f * _rms_scale(f) * pg_ref[...]


def _ffn(x, g, w1, w2, pg, layer, tm, tf):
    t = x.shape[0]
    return pl.pallas_call(
        _ffn_kernel,
        grid=(t // tm, D_FF // tf),
        in_specs=[pl.BlockSpec((tm, D_MODEL), lambda i, j: (i, 0)),
                  _resident((1, D_MODEL)),
                  pl.BlockSpec((None, D_MODEL, tf), lambda i, j: (layer, 0, j)),
                  pl.BlockSpec((None, tf, D_MODEL), lambda i, j: (layer, j, 0)),
                  _resident((1, D_MODEL))],
        out_specs=pl.BlockSpec((tm, D_MODEL), lambda i, j: (i, 0)),
        out_shape=jax.ShapeDtypeStruct((t, D_MODEL), F32),
        scratch_shapes=[pltpu.VMEM((tm, D_MODEL), BF16)],
        compiler_params=_params("parallel", "arbitrary"),
    )(x, g, w1, w2, pg)


def _prep_layer(pre_mix_g, w_in, q_norm_g, w_q_up, kv_norm_g, w_kv_up, conv_w, conv_b,
                dt_bias_f, dt_bias_b, a_log_f, a_log_b, d_skip, ssd_norm_g, w_out,
                post_mix_g, pre_ffn_g, w_ff1, w_ff2, post_ffn_g):
    o_cq = FNET_DIM
    o_ckv = o_cq + Q_LORA
    o_pe = o_ckv + KV_LORA
    o_z = o_pe + QK_ROPE
    o_xbc = o_z + SSD_DIM
    o_dt = o_xbc + CONV_DIM
    zeros = lambda n: jnp.zeros((D_MODEL, n), w_in.dtype)
    small_cols = jnp.concatenate([
        w_in[:, o_pe:o_pe + ROPE_HALF],
        w_in[:, o_dt:o_dt + 2 * SSD_HEADS],
        zeros(SMALL_PE2 - ROPE_HALF - 2 * SSD_HEADS),
        w_in[:, o_pe + ROPE_HALF:o_pe + QK_ROPE],
        zeros(LANES - SMALL_PE2 - ROPE_HALF)], axis=1)
    w_in_r = jnp.concatenate([w_in[:, :o_pe], w_in[:, o_z:o_dt], small_cols], axis=1).astype(BF16)

    wq = w_q_up.reshape(Q_LORA, MLA_HEADS, QK_HEAD)
    zq = jnp.zeros((Q_LORA, MLA_HEADS, ROPE_HALF), w_q_up.dtype)
    wq = jnp.concatenate([wq[..., :QK_NOPE + ROPE_HALF], zq, wq[..., QK_NOPE + ROPE_HALF:], zq], axis=-1)
    wq_t = wq.reshape(Q_LORA, MLA_HEADS * QK_PAD).T.astype(BF16)
    wkv = w_kv_up.reshape(KV_LORA, MLA_HEADS, QK_NOPE + V_HEAD)
    wk = wkv[..., :QK_NOPE].reshape(KV_LORA, MLA_HEADS * QK_NOPE).astype(BF16)
    wv_t = wkv[..., QK_NOPE:].reshape(KV_LORA, MLA_DIM).T.astype(BF16)

    row = lambda v: v.astype(F32).reshape(1, -1)
    return dict(
        pre_mix_g=row(pre_mix_g), w_in=w_in_r, q_norm_g=row(q_norm_g), kv_norm_g=row(kv_norm_g),
        wq_t=wq_t, wk=wk, wv_t=wv_t,
        conv_w=jnp.pad(conv_w.astype(F32), ((0, 8 - CONV_K), (0, 0))), conv_b=row(conv_b),
        dt_bias_f=dt_bias_f, dt_bias_b=dt_bias_b, a_log_f=a_log_f, a_log_b=a_log_b,
        d_skip=row(jnp.repeat(d_skip, SSD_HEAD_DIM)), ssd_norm_g=row(ssd_norm_g),
        w_out=w_out.astype(BF16), post_mix_g=row(post_mix_g), pre_ffn_g=row(pre_ffn_g),
        post_ffn_g=row(post_ffn_g))


def _tile(n, want):
    return want if n % want == 0 else n


def _layer(x, p):
    b, s, _ = x.shape
    t = b * s
    xt = x.reshape(t, D_MODEL)
    uf, cq, ckv, z, xbc, small = _inproj(xt, p["pre_mix_g"], p["w_in"], _tile(t, 512))
    small3 = small.reshape(b, s, SMALL)

    fo = _fourier(uf.reshape(b, s, FNET_DIM)).reshape(t, FNET_DIM)

    q_t, k, v_t = _mla_prep(cq.reshape(b, s, Q_LORA), ckv.reshape(b, s, KV_LORA), small3,
                            p["q_norm_g"], p["kv_norm_g"], p["wq_t"], p["wk"], p["wv_t"],
                            _tile(s, 512))
    mo = _flash(q_t, k, v_t, _tile(s, 1024)).reshape(t, MLA_DIM)

    xs, bm, cm = _conv(xbc.reshape(b, s, CONV_DIM), p["conv_w"], p["conv_b"], _tile(s, 512))
    chunks = 8 if s % (8 * CHUNK) == 0 else 1
    yf, yb = _ssd(xs, bm, cm, small3, p["dt_bias_f"], p["dt_bias_b"], p["a_log_f"], p["a_log_b"],
                  chunks)

    x1 = _outproj(xt, fo, mo, yf.reshape(t, SSD_DIM), yb.reshape(t, SSD_DIM),
                  xs.reshape(t, SSD_DIM), z, p["d_skip"], p["ssd_norm_g"], p["w_out"],
                  p["post_mix_g"], _tile(t, 512))
    x2 = _ffn(x1, p["pre_ffn_g"], p["w_ff1"], p["w_ff2"], p["post_ffn_g"], p["layer"],
              _tile(t, 512), 2048)
    return x2.reshape(b, s, D_MODEL)


def kernel(x_prompt, x_sample, pre_mix_g, w_in, q_norm_g, w_q_up, kv_norm_g, w_kv_up, conv_w, conv_b, dt_bias_f, dt_bias_b, a_log_f, a_log_b, d_skip, ssd_norm_g, w_out, post_mix_g, pre_ffn_g, w_ff1, w_ff2, post_ffn_g):
    weights = (pre_mix_g, w_in, q_norm_g, w_q_up, kv_norm_g, w_kv_up, conv_w, conv_b,
               dt_bias_f, dt_bias_b, a_log_f, a_log_b, d_skip, ssd_norm_g, w_out,
               post_mix_g, pre_ffn_g, w_ff1, w_ff2, post_ffn_g)
    depth = w_in.shape[0]
    layers = [_prep_layer(*[w[l] for w in weights]) for l in range(depth)]
    w_ff1_b, w_ff2_b = w_ff1.astype(BF16), w_ff2.astype(BF16)
    for l, p in enumerate(layers):
        p.update(layer=l, w_ff1=w_ff1_b, w_ff2=w_ff2_b)

    def run_trunk(x):
        for p in layers:
            x = _layer(x, p)
        return x

    return (run_trunk(x_prompt), run_trunk(x_sample))
```
